```python
import jax
import jax.numpy as jnp
from jax import lax
import numpy as np

D_MODEL = 1024
BATCH = 8
SEQ = 2048
DEPTH = 2
DEC_BATCH = 1
DEC_SEQ = 16384
PAST_LEN = 128

N_MIXERS = 2
N_GLA_LAYERS = (DEPTH + 1) // 2
N_MLA_LAYERS = DEPTH // 2
EPS = 1e-6

GLA_HEADS = 4
GLA_DK = D_MODEL // (2 * GLA_HEADS)
GLA_DV = D_MODEL // GLA_HEADS
GLA_KEY_W = GLA_HEADS * GLA_DK
GLA_VAL_W = GLA_HEADS * GLA_DV
GLA_GATE_RANK = 16
GLA_GATE_TAU = 16.0
GLA_CHUNK = 64
GLA_IN_W = 2 * GLA_KEY_W + 2 * GLA_VAL_W + 2 * GLA_GATE_RANK

MLA_HEADS = 16
MLA_Q_RANK = 384
MLA_KV_RANK = 256
MLA_NOPE = 128
MLA_ROPE = 64
MLA_V = 128
MLA_QK = MLA_NOPE + MLA_ROPE
MLA_IN_W = MLA_Q_RANK + MLA_KV_RANK + MLA_ROPE
ROPE_THETA = 10000.0
Q_BLOCK = 128

D_FF = 2816
CONV_W = 3

kernel_name = "hybrid_gla_mla_convffn_encoder"


def rms_norm(x, g):
    xf = x.astype(jnp.float32)
    y = xf * lax.rsqrt(jnp.mean(xf * xf, axis=-1, keepdims=True) + EPS)
    return (y * g.astype(jnp.float32)).astype(x.dtype)


def _to_chunks(t, dh):
    b, s, _ = t.shape
    t = t.reshape(b, s // GLA_CHUNK, GLA_CHUNK, -1, dh)
    return t.transpose(0, 3, 1, 2, 4)


def _from_chunks(t):
    b, h, n, c, d = t.shape
    return t.transpose(0, 2, 3, 1, 4).reshape(b, n * c, h, d)


def gla_direction(q, k, v, log_a, strict):
    cum = jnp.cumsum(log_a, axis=3)
    cum_last = cum[:, :, :, -1:, :]
    q_dec = q * jnp.exp(cum)
    k_inv = k * jnp.exp(-cum)
    scores = jnp.einsum("bhncd,bhnsd->bhncs", q_dec, k_inv)
    mask = jnp.tril(jnp.ones((GLA_CHUNK, GLA_CHUNK), dtype=bool), -1 if strict else 0)
    o_intra = jnp.einsum("bhncs,bhnse->bhnce", jnp.where(mask, scores, 0.0), v)
    k_end = k * jnp.exp(cum_last - cum)
    chunk_kv = jnp.einsum("bhnsd,bhnse->bhnde", k_end, v)
    chunk_decay = jnp.exp(cum_last[:, :, :, 0, :])

    def step(state, inp):
        q_n, kv_n, dec_n = inp
        o_n = jnp.einsum("bhcd,bhde->bhce", q_n, state)
        return dec_n[..., None] * state + kv_n, o_n

    b, h, _, _, dk = q.shape
    state0 = jnp.zeros((b, h, dk, v.shape[-1]), jnp.float32)
    _, o_inter = lax.scan(step, state0, (jnp.moveaxis(q_dec, 2, 0), jnp.moveaxis(chunk_kv, 2, 0), jnp.moveaxis(chunk_decay, 2, 0)))
    return o_intra + jnp.moveaxis(o_inter, 0, 2)


def gla_mixer(h, w_in, w_gate_up, b_gate, g_out, w_out):
    b, s, _ = h.shape
    proj = h @ w_in
    o1 = GLA_KEY_W
    o2 = 2 * GLA_KEY_W
    o3 = o2 + GLA_VAL_W
    o4 = o3 + GLA_VAL_W
    q = proj[..., :o1].astype(jnp.float32) * (GLA_DK ** -0.5)
    k = proj[..., o1:o2].astype(jnp.float32)
    v = proj[..., o2:o3].astype(jnp.float32)
    r = proj[..., o3:o4]
    g_low = proj[..., o4:].reshape(b, s, 2, GLA_GATE_RANK).astype(jnp.float32)
    log_a = jax.nn.log_sigmoid(jnp.einsum("bsjr,jrk->bsjk", g_low, w_gate_up.astype(jnp.float32)) + b_gate.astype(jnp.float32)) / GLA_GATE_TAU
    fwd = gla_direction(_to_chunks(q, GLA_DK), _to_chunks(k, GLA_DK), _to_chunks(v, GLA_DV), _to_chunks(log_a[:, :, 0], GLA_DK), strict=False)
    qr, kr, vr, ar = (jnp.flip(t, axis=1) for t in (q, k, v, log_a[:, :, 1]))
    bwd = gla_direction(_to_chunks(qr, GLA_DK), _to_chunks(kr, GLA_DK), _to_chunks(vr, GLA_DV), _to_chunks(ar, GLA_DK), strict=True)
    o = _from_chunks(fwd) + jnp.flip(_from_chunks(bwd), axis=1)
    o = o * lax.rsqrt(jnp.mean(o * o, axis=-1, keepdims=True) + EPS) * g_out.astype(jnp.float32).reshape(GLA_HEADS, GLA_DV)
    o = o.reshape(b, s, GLA_VAL_W).astype(h.dtype) * jax.nn.silu(r)
    return o @ w_out


def rope(x, pos):
    half = MLA_ROPE // 2
    freqs = ROPE_THETA ** (-jnp.arange(half, dtype=jnp.float32) / half)
    ang = pos[:, None] * freqs[None, :]
    cos = jnp.cos(ang)[None, :, None, :]
    sin = jnp.sin(ang)[None, :, None, :]
    xf = x.astype(jnp.float32)
    x1, x2 = xf[..., :half], xf[..., half:]
    return jnp.concatenate([x1 * cos - x2 * sin, x2 * cos + x1 * sin], axis=-1).astype(x.dtype)


def block_attention(q, k, v):
    b, s, h, _ = q.shape
    nb = s // Q_BLOCK
    qb = q.reshape(b, nb, Q_BLOCK, h, MLA_QK).transpose(1, 0, 2, 3, 4)

    def one(q_blk):
        sc = jnp.einsum("bqhd,bkhd->bhqk", q_blk, k).astype(jnp.float32) * (MLA_QK ** -0.5)
        p = jax.nn.softmax(sc, axis=-1).astype(v.dtype)
        return jnp.einsum("bhqk,bkhd->bqhd", p, v)

    o = lax.map(one, qb)
    return o.transpose(1, 0, 2, 3, 4).reshape(b, s, h * MLA_V)


def mla_mixer(h, w_in, g_q, w_uq, g_kv, w_ukv, w_out):
    b, s, _ = h.shape
    proj = h @ w_in
    c_q = proj[..., :MLA_Q_RANK]
    c_kv = proj[..., MLA_Q_RANK:MLA_Q_RANK + MLA_KV_RANK]
    k_rope = proj[..., MLA_Q_RANK + MLA_KV_RANK:]
    q = (rms_norm(c_q, g_q) @ w_uq).reshape(b, s, MLA_HEADS, MLA_QK)
    kv = (rms_norm(c_kv, g_kv) @ w_ukv).reshape(b, s, MLA_HEADS, MLA_NOPE + MLA_V)
    pos = jnp.arange(s, dtype=jnp.float32)
    q = jnp.concatenate([q[..., :MLA_NOPE], rope(q[..., MLA_NOPE:], pos)], axis=-1)
    k_r = jnp.broadcast_to(rope(k_rope[:, :, None, :], pos), (b, s, MLA_HEADS, MLA_ROPE))
    k = jnp.concatenate([kv[..., :MLA_NOPE], k_r], axis=-1)
    v = kv[..., MLA_NOPE:]
    return block_attention(q, k, v) @ w_out


def conv_ffn(h, w_up, conv_w, conv_b, w_down):
    u = h @ w_up
    u = lax.conv_general_dilated(u, conv_w, window_strides=(1,), padding=((CONV_W // 2, CONV_W // 2),), dimension_numbers=("NWC", "WIO", "NWC"), feature_group_count=2 * D_FF) + conv_b
    a, g = u[..., :D_FF], u[..., D_FF:]
    return (a * jax.nn.silu(g)) @ w_down


def trunk(x, norm_mix, norm_ffn, norm_final, gla_w_in, gla_w_gate_up, gla_b_gate, gla_g_out, gla_w_out, mla_w_in, mla_g_q, mla_w_uq, mla_g_kv, mla_w_ukv, mla_w_out, ffn_w_up, ffn_conv_w, ffn_conv_b, ffn_w_down):
    for i in range(DEPTH):
        h = rms_norm(x, norm_mix[i])
        j = i // N_MIXERS
        if i % N_MIXERS == 0:
            x = x + gla_mixer(h, gla_w_in[j], gla_w_gate_up[j], gla_b_gate[j], gla_g_out[j], gla_w_out[j])
        else:
            x = x + mla_mixer(h, mla_w_in[j], mla_g_q[j], mla_w_uq[j], mla_g_kv[j], mla_w_ukv[j], mla_w_out[j])
        h = rms_norm(x, norm_ffn[i])
        x = x + conv_ffn(h, ffn_w_up[i], ffn_conv_w[i], ffn_conv_b[i], ffn_w_down[i])
    return rms_norm(x, norm_final)


def setup_inputs(seed: int = 0) -> dict:
    key = jax.random.key(seed)
    ks = jax.random.split(key, 20)

    def nrm(k, shape, scale):
        return jax.random.normal(k, shape, jnp.float32) * scale

    def gain(k, shape):
        return 1.0 + 0.01 * jax.random.normal(k, shape, jnp.float32)

    res_scale = (2 * DEPTH) ** -0.5
    return {
        "x_prompt": nrm(ks[0], (BATCH, SEQ, D_MODEL), 1.0),
        "x_sample": nrm(ks[1], (DEC_BATCH, DEC_SEQ, D_MODEL), 1.0),
        "norm_mix": gain(ks[2], (DEPTH, D_MODEL)),
        "norm_ffn": gain(ks[3], (DEPTH, D_MODEL)),
        "norm_final": gain(ks[4], (D_MODEL,)),
        "gla_w_in": nrm(ks[5], (N_GLA_LAYERS, D_MODEL, GLA_IN_W), D_MODEL ** -0.5),
        "gla_w_gate_up": nrm(ks[6], (N_GLA_LAYERS, 2, GLA_GATE_RANK, GLA_KEY_W), GLA_GATE_RANK ** -0.5),
        "gla_b_gate": nrm(ks[7], (N_GLA_LAYERS, 2, GLA_KEY_W), 0.1),
        "gla_g_out": gain(ks[8], (N_GLA_LAYERS, GLA_VAL_W)),
        "gla_w_out": nrm(ks[9], (N_GLA_LAYERS, GLA_VAL_W, D_MODEL), GLA_VAL_W ** -0.5 * res_scale),
        "mla_w_in": nrm(ks[10], (N_MLA_LAYERS, D_MODEL, MLA_IN_W), D_MODEL ** -0.5),
        "mla_g_q": gain(ks[11], (N_MLA_LAYERS, MLA_Q_RANK)),
        "mla_w_uq": nrm(ks[12], (N_MLA_LAYERS, MLA_Q_RANK, MLA_HEADS * MLA_QK), MLA_Q_RANK ** -0.5),
        "mla_g_kv": gain(ks[13], (N_MLA_LAYERS, MLA_KV_RANK)),
        "mla_w_ukv": nrm(ks[14], (N_MLA_LAYERS, MLA_KV_RANK, MLA_HEADS * (MLA_NOPE + MLA_V)), MLA_KV_RANK ** -0.5),
        "mla_w_out": nrm(ks[15], (N_MLA_LAYERS, MLA_HEADS * MLA_V, D_MODEL), (MLA_HEADS * MLA_V) ** -0.5 * res_scale),
        "ffn_w_up": nrm(ks[16], (DEPTH, D_MODEL, 2 * D_FF), D_MODEL ** -0.5),
        "ffn_conv_w": nrm(ks[17], (DEPTH, CONV_W, 1, 2 * D_FF), CONV_W ** -0.5),
        "ffn_conv_b": nrm(ks[18], (DEPTH, 2 * D_FF), 0.01),
        "ffn_w_down": nrm(ks[19], (DEPTH, D_FF, D_MODEL), D_FF ** -0.5 * res_scale),
    }


def reference(x_prompt, x_sample, norm_mix, norm_ffn, norm_final, gla_w_in, gla_w_gate_up, gla_b_gate, gla_g_out, gla_w_out, mla_w_in, mla_g_q, mla_w_uq, mla_g_kv, mla_w_ukv, mla_w_out, ffn_w_up, ffn_conv_w, ffn_conv_b, ffn_w_down):
    y_prompt = trunk(x_prompt, norm_mix, norm_ffn, norm_final, gla_w_in, gla_w_gate_up, gla_b_gate, gla_g_out, gla_w_out, mla_w_in, mla_g_q, mla_w_uq, mla_g_kv, mla_w_ukv, mla_w_out, ffn_w_up, ffn_conv_w, ffn_conv_b, ffn_w_down)
    y_sample = trunk(x_sample, norm_mix, norm_ffn, norm_final, gla_w_in, gla_w_gate_up, gla_b_gate, gla_g_out, gla_w_out, mla_w_in, mla_g_q, mla_w_uq, mla_g_kv, mla_w_ukv, mla_w_out, ffn_w_up, ffn_conv_w, ffn_conv_b, ffn_w_down)
    return (y_prompt, y_sample)
```

```python
import functools
import math

import jax
import jax.numpy as jnp
from jax import lax
from jax.experimental import pallas as pl
from jax.experimental.pallas import tpu as pltpu

F32 = jnp.float32
BF16 = jnp.bfloat16

D_MODEL = 1024
EPS = 1e-6

GLA_HEADS = 4
GLA_DK = 128
GLA_DV = 256
GLA_KEY_W = GLA_HEADS * GLA_DK
GLA_VAL_W = GLA_HEADS * GLA_DV
GLA_GATE_RANK = 16
GLA_GATE_TAU = 16.0
GLA_CHUNK = 64
GLA_IN_W = 2 * GLA_KEY_W + 2 * GLA_VAL_W + 2 * GLA_GATE_RANK
GLA_IN_PAD = 3200
GLA_GATE_COL = 2 * GLA_KEY_W + 2 * GLA_VAL_W

MLA_HEADS = 16
MLA_Q_RANK = 384
MLA_KV_RANK = 256
MLA_NOPE = 128
MLA_ROPE = 64
MLA_V = 128
MLA_QK = MLA_NOPE + MLA_ROPE
MLA_IN_W = MLA_Q_RANK + MLA_KV_RANK + MLA_ROPE
MLA_IN_EXT = 896
ROPE_THETA = 10000.0
ROPE_LANES = 128
HEAD_PAD = 256

D_FF = 2816
FF_CHUNK = 256
N_FF_CHUNKS = D_FF // FF_CHUNK

SUBLANES = 8
VMEM_LIMIT = 56 * 1024 * 1024

TOK_TILE = 512
GLA_TILE = 256
ATT_TQ = 512
ATT_TK = 512


def _rms(x, g):
    return x * lax.rsqrt(jnp.mean(x * x, axis=-1, keepdims=True) + EPS) * g


def _params(sem):
    return pltpu.CompilerParams(dimension_semantics=sem, vmem_limit_bytes=VMEM_LIMIT)


def _norm_matmul_kernel(x_ref, g_ref, w_ref, o_ref):
    hn = _rms(x_ref[...], g_ref[...]).astype(BF16)
    o_ref[...] = jnp.dot(hn, w_ref[...], preferred_element_type=F32).astype(o_ref.dtype)


def norm_matmul(x, g, w, name):
    n_tok, d = x.shape
    n_out = w.shape[1]
    return pl.pallas_call(
        _norm_matmul_kernel,
        grid=(n_tok // TOK_TILE,),
        in_specs=[
            pl.BlockSpec((TOK_TILE, d), lambda i: (i, 0)),
            pl.BlockSpec((1, d), lambda i: (0, 0)),
            pl.BlockSpec((d, n_out), lambda i: (0, 0)),
        ],
        out_specs=pl.BlockSpec((TOK_TILE, n_out), lambda i: (i, 0)),
        out_shape=jax.ShapeDtypeStruct((n_tok, n_out), BF16),
        compiler_params=_params(("parallel",)),
        name=name,
    )(x, g, w)


def _matmul_residual_kernel(x_ref, a_ref, w_ref, o_ref):
    o_ref[...] = x_ref[...] + jnp.dot(a_ref[...], w_ref[...], preferred_element_type=F32)


def matmul_residual(x, a, w, name):
    n_tok, d = x.shape
    k = a.shape[1]
    return pl.pallas_call(
        _matmul_residual_kernel,
        grid=(n_tok // TOK_TILE,),
        in_specs=[
            pl.BlockSpec((TOK_TILE, d), lambda i: (i, 0)),
            pl.BlockSpec((TOK_TILE, k), lambda i: (i, 0)),
            pl.BlockSpec((k, d), lambda i: (0, 0)),
        ],
        out_specs=pl.BlockSpec((TOK_TILE, d), lambda i: (i, 0)),
        out_shape=jax.ShapeDtypeStruct((n_tok, d), F32),
        compiler_params=_params(("parallel",)),
        name=name,
    )(x, a, w)


def _gla_kernel(*refs, bwd, final):
    if final:
        (q_ref, k_ref, v_ref, g_ref, wg_ref, bg_ref, r_ref, ob_ref, gout_ref, wout_ref, x_ref,
         o_ref, state_ref, gated_ref) = refs
    else:
        q_ref, k_ref, v_ref, g_ref, wg_ref, bg_ref, o_ref, state_ref = refs
    t = GLA_TILE
    n_chunks = t // GLA_CHUNK

    @pl.when(pl.program_id(1) == 0)
    def _():
        state_ref[...] = jnp.zeros_like(state_ref)

    z = jnp.dot(g_ref[...], wg_ref[...], preferred_element_type=F32) + bg_ref[...]
    la = -(jnp.maximum(-z, 0.0) + jnp.log1p(jnp.exp(-jnp.abs(z)))) * (1.0 / GLA_GATE_TAU)

    rows = lax.broadcasted_iota(jnp.int32, (t, t), 0)
    cols = lax.broadcasted_iota(jnp.int32, (t, t), 1)
    shift = GLA_CHUNK.bit_length() - 1
    same = (rows >> shift) == (cols >> shift)
    if bwd:
        cum_m = same & (cols >= rows)
        att_m = same & (cols > rows)
    else:
        cum_m = same & (cols <= rows)
        att_m = same & (cols <= rows)
    cum_mat = jnp.where(cum_m, 1.0, 0.0).astype(BF16)
    tot_mat = jnp.where(same, 1.0, 0.0).astype(BF16)

    la_hi = la.astype(BF16)
    la_lo = (la - la_hi.astype(F32)).astype(BF16)
    cum = (jnp.dot(cum_mat, la_hi, preferred_element_type=F32)
           + jnp.dot(cum_mat, la_lo, preferred_element_type=F32))
    tot = (jnp.dot(tot_mat, la_hi, preferred_element_type=F32)
           + jnp.dot(tot_mat, la_lo, preferred_element_type=F32))

    q = q_ref[...].astype(F32) * (GLA_DK ** -0.5)
    k = k_ref[...].astype(F32)
    q_dec = (q * jnp.exp(cum)).astype(BF16)
    k_inv = (k * jnp.exp(-cum)).astype(BF16)
    k_end = (k * jnp.exp(tot - cum)).astype(BF16)
    decay = jnp.exp(tot)

    order = range(n_chunks - 1, -1, -1) if bwd else range(n_chunks)
    for h in range(GLA_HEADS):
        ks = slice(h * GLA_DK, (h + 1) * GLA_DK)
        vs = slice(h * GLA_DV, (h + 1) * GLA_DV)
        qd = q_dec[:, ks]
        vh = v_ref[:, vs]
        sc = lax.dot_general(qd, k_inv[:, ks], (((1,), (1,)), ((), ())), preferred_element_type=F32)
        p = jnp.where(att_m, sc, 0.0).astype(BF16)
        o_intra = jnp.dot(p, vh, preferred_element_type=F32)
        st = state_ref[h]
        for n in order:
            rs = slice(n * GLA_CHUNK, (n + 1) * GLA_CHUNK)
            o_inter = lax.dot_general(qd[rs], st.astype(BF16), (((1,), (1,)), ((), ())),
                                      preferred_element_type=F32)
            kv_t = lax.dot_general(vh[rs], k_end[rs, ks], (((0,), (0,)), ((), ())),
                                   preferred_element_type=F32)
            st = decay[n * GLA_CHUNK:n * GLA_CHUNK + 1, ks] * st + kv_t
            o_n = o_intra[rs] + o_inter
            if final:
                o_n = o_n + ob_ref[rs, vs]
                o_n = o_n * lax.rsqrt(jnp.mean(o_n * o_n, axis=-1, keepdims=True) + EPS) * gout_ref[:, vs]
                r = r_ref[rs, vs].astype(F32)
                o_n = o_n * (r / (1.0 + jnp.exp(-r)))
                gated_ref[rs, vs] = o_n.astype(BF16)
            else:
                o_ref[rs, vs] = o_n
        state_ref[h] = st

    if final:
        o_ref[...] = x_ref[...] + jnp.dot(gated_ref[...], wout_ref[...], preferred_element_type=F32)


def gla_pass(proj, wg, bg, batch, seq, *, bwd, extra=None):
    n_tok = proj.shape[0]
    t = GLA_TILE
    nb = seq // t
    final = extra is not None

    def row(b, j):
        return b * nb + ((nb - 1 - j) if bwd else j)

    def col(c):
        return lambda b, j: (row(b, j), c)

    in_specs = [
        pl.BlockSpec((t, GLA_KEY_W), col(0)),
        pl.BlockSpec((t, GLA_KEY_W), col(1)),
        pl.BlockSpec((t, GLA_VAL_W), col(1)),
        pl.BlockSpec((t, ROPE_LANES), col(GLA_GATE_COL // ROPE_LANES)),
        pl.BlockSpec((ROPE_LANES, GLA_KEY_W), lambda b, j: (0, 0)),
        pl.BlockSpec((1, GLA_KEY_W), lambda b, j: (0, 0)),
    ]
    args = [proj, proj, proj, proj, wg, bg]
    scratch = [pltpu.VMEM((GLA_HEADS, GLA_DV, GLA_DK), F32)]
    if final:
        o_bwd, g_out, w_out, x = extra
        in_specs += [
            pl.BlockSpec((t, GLA_VAL_W), col(2)),
            pl.BlockSpec((t, GLA_VAL_W), col(0)),
            pl.BlockSpec((1, GLA_VAL_W), lambda b, j: (0, 0)),
            pl.BlockSpec((GLA_VAL_W, D_MODEL), lambda b, j: (0, 0)),
            pl.BlockSpec((t, D_MODEL), col(0)),
        ]
        args += [proj, o_bwd, g_out, w_out, x]
        scratch.append(pltpu.VMEM((t, GLA_VAL_W), BF16))
    return pl.pallas_call(
        functools.partial(_gla_kernel, bwd=bwd, final=final),
        grid=(batch, nb),
        in_specs=in_specs,
        out_specs=pl.BlockSpec((t, GLA_VAL_W), col(0)),
        out_shape=jax.ShapeDtypeStruct((n_tok, GLA_VAL_W), F32),
        scratch_shapes=scratch,
        compiler_params=_params(("parallel", "arbitrary")),
        name="gla_fwd_out" if final else "gla_bwd",
    )(*args)


def _ffn_kernel(*refs, tiles_per_seq, final):
    if final:
        (x_ref, xp_ref, xn_ref, g_ref, wup_ref, cpar_ref, wdown_ref, gfin_ref, o_ref,
         hn_ref, hnb_ref, acc_ref) = refs
    else:
        x_ref, xp_ref, xn_ref, g_ref, wup_ref, cpar_ref, wdown_ref, o_ref, hn_ref, hnb_ref, acc_ref = refs
    t = TOK_TILE
    ext = t + 2 * SUBLANES
    i = pl.program_id(0)
    g = g_ref[...]
    keep_prev = jnp.where(i % tiles_per_seq == 0, 0.0, 1.0)
    keep_next = jnp.where((i + 1) % tiles_per_seq == 0, 0.0, 1.0)
    hn_ref[0:SUBLANES, :] = _rms(xp_ref[...], g) * keep_prev
    hn_ref[SUBLANES:SUBLANES + t, :] = _rms(x_ref[...], g)
    hn_ref[SUBLANES + t:ext, :] = _rms(xn_ref[...], g) * keep_next
    hnb_ref[...] = hn_ref[...].astype(BF16)
    acc_ref[...] = x_ref[...]

    def body(c, carry):
        u = jnp.dot(hnb_ref[...], wup_ref[c], preferred_element_type=F32)
        cp = cpar_ref[c]
        conv = (cp[0:1] * pltpu.roll(u, 1, 0) + cp[1:2] * u
                + cp[2:3] * pltpu.roll(u, ext - 1, 0) + cp[3:4])
        conv = conv[SUBLANES:SUBLANES + t]
        a = conv[:, :FF_CHUNK]
        gt = conv[:, FF_CHUNK:]
        act = (a * (gt / (1.0 + jnp.exp(-gt)))).astype(BF16)
        acc_ref[...] += jnp.dot(act, wdown_ref[c], preferred_element_type=F32)
        return carry

    lax.fori_loop(0, N_FF_CHUNKS, body, 0)
    out = acc_ref[...]
    if final:
        out = _rms(out, gfin_ref[...])
    o_ref[...] = out


def conv_ffn(x, g, wup, cpar, wdown, seq, g_final=None):
    n_tok, d = x.shape
    t = TOK_TILE
    r8 = t // SUBLANES
    last8 = n_tok // SUBLANES - 1
    final = g_final is not None
    const3 = lambda i: (0, 0, 0)
    in_specs = [
        pl.BlockSpec((t, d), lambda i: (i, 0)),
        pl.BlockSpec((SUBLANES, d), lambda i: (jnp.maximum(i * r8 - 1, 0), 0)),
        pl.BlockSpec((SUBLANES, d), lambda i: (jnp.minimum((i + 1) * r8, last8), 0)),
        pl.BlockSpec((1, d), lambda i: (0, 0)),
        pl.BlockSpec(wup.shape, const3, pipeline_mode=pl.Buffered(1)),
        pl.BlockSpec(cpar.shape, const3, pipeline_mode=pl.Buffered(1)),
        pl.BlockSpec(wdown.shape, const3, pipeline_mode=pl.Buffered(1)),
    ]
    args = [x, x, x, g, wup, cpar, wdown]
    if final:
        in_specs.append(pl.BlockSpec((1, d), lambda i: (0, 0)))
        args.append(g_final)
    return pl.pallas_call(
        functools.partial(_ffn_kernel, tiles_per_seq=seq // t, final=final),
        grid=(n_tok // t,),
        in_specs=in_specs,
        out_specs=pl.BlockSpec((t, d), lambda i: (i, 0)),
        out_shape=jax.ShapeDtypeStruct((n_tok, d), F32),
        scratch_shapes=[pltpu.VMEM((t + 2 * SUBLANES, d), F32), pltpu.VMEM((t + 2 * SUBLANES, d), BF16),
                        pltpu.VMEM((t, d), F32)],
        compiler_params=_params(("parallel",)),
        name="conv_ffn_final" if final else "conv_ffn",
    )(*args)


def _mla_qkv_kernel(p_ref, gq_ref, gkv_ref, freq_ref, wn_ref, wr_ref, wrr_ref, wk_ref, wvt_ref,
                    q_ref, k_ref, vt_ref, *, tiles_per_seq):
    t = TOK_TILE
    i = pl.program_id(0)
    pos0 = (i % tiles_per_seq) * t
    pos = (pos0 + lax.broadcasted_iota(jnp.int32, (t, ROPE_LANES), 0)).astype(F32)
    ang = pos * freq_ref[...]
    cos = jnp.cos(ang)
    sin = jnp.sin(ang)

    cq = _rms(p_ref[:, 0:MLA_Q_RANK].astype(F32), gq_ref[...]).astype(BF16)
    ckv = _rms(p_ref[:, MLA_Q_RANK:MLA_Q_RANK + MLA_KV_RANK].astype(F32), gkv_ref[...]).astype(BF16)
    c0 = MLA_Q_RANK + MLA_KV_RANK
    kr = (p_ref[:, c0:c0 + ROPE_LANES].astype(F32) * cos
          + p_ref[:, c0 + ROPE_LANES:c0 + 2 * ROPE_LANES].astype(F32) * sin).astype(BF16)

    scale = (MLA_QK ** -0.5) * math.log2(math.e)
    qn = jnp.dot(cq, wn_ref[...], preferred_element_type=F32) * scale
    qr = jnp.dot(cq, wr_ref[...], preferred_element_type=F32)
    qrr = jnp.dot(cq, wrr_ref[...], preferred_element_type=F32)
    kn = jnp.dot(ckv, wk_ref[...], preferred_element_type=F32)
    vt = lax.dot_general(wvt_ref[...], ckv, (((1,), (1,)), ((), ())), preferred_element_type=F32)
    for h in range(MLA_HEADS):
        hs = slice(h * MLA_NOPE, (h + 1) * MLA_NOPE)
        q_ref[h, :, 0:MLA_NOPE] = qn[:, hs].astype(BF16)
        q_ref[h, :, MLA_NOPE:HEAD_PAD] = ((qr[:, hs] * cos + qrr[:, hs] * sin) * scale).astype(BF16)
        k_ref[h, :, 0:MLA_NOPE] = kn[:, hs].astype(BF16)
        k_ref[h, :, MLA_NOPE:HEAD_PAD] = kr
        vt_ref[h, 0] = vt[hs, :].astype(BF16)


def mla_qkv(proj, gq, gkv, freqs, wn, wr, wrr, wk, wvt, seq):
    n_tok = proj.shape[0]
    t = TOK_TILE
    nt = n_tok // t
    c2 = lambda i: (0, 0)
    return pl.pallas_call(
        functools.partial(_mla_qkv_kernel, tiles_per_seq=seq // t),
        grid=(nt,),
        in_specs=[
            pl.BlockSpec((t, MLA_IN_EXT), lambda i: (i, 0)),
            pl.BlockSpec(gq.shape, c2),
            pl.BlockSpec(gkv.shape, c2),
            pl.BlockSpec(freqs.shape, c2),
            pl.BlockSpec(wn.shape, c2),
            pl.BlockSpec(wr.shape, c2),
            pl.BlockSpec(wrr.shape, c2),
            pl.BlockSpec(wk.shape, c2),
            pl.BlockSpec(wvt.shape, c2),
        ],
        out_specs=[
            pl.BlockSpec((MLA_HEADS, t, HEAD_PAD), lambda i: (0, i, 0)),
            pl.BlockSpec((MLA_HEADS, t, HEAD_PAD), lambda i: (0, i, 0)),
            pl.BlockSpec((MLA_HEADS, 1, MLA_V, t), lambda i: (0, i, 0, 0)),
        ],
        out_shape=[
            jax.ShapeDtypeStruct((MLA_HEADS, n_tok, HEAD_PAD), BF16),
            jax.ShapeDtypeStruct((MLA_HEADS, n_tok, HEAD_PAD), BF16),
            jax.ShapeDtypeStruct((MLA_HEADS, nt, MLA_V, t), BF16),
        ],
        compiler_params=_params(("parallel",)),
        name="mla_qkv",
    )(proj, gq, gkv, freqs, wn, wr, wrr, wk, wvt)


def _attn_kernel(q_ref, k_ref, vt_ref, o_ref, *, n_kv):
    q = q_ref[...]

    def body(j, carry):
        m, l, acc = carry
        kj = k_ref[pl.ds(pl.multiple_of(j * ATT_TK, ATT_TK), ATT_TK), :]
        s = lax.dot_general(kj, q, (((1,), (1,)), ((), ())), preferred_element_type=F32)
        m_new = jnp.maximum(m, jnp.max(s, axis=0, keepdims=True))
        alpha = jnp.exp2(m - m_new)
        p = jnp.exp2(s - m_new)
        l = alpha * l + jnp.sum(p, axis=0, keepdims=True)
        acc = alpha * acc + jnp.dot(vt_ref[j], p.astype(BF16), preferred_element_type=F32)
        return m_new, l, acc

    m0 = jnp.full((1, ATT_TQ), -jnp.inf, F32)
    l0 = jnp.zeros((1, ATT_TQ), F32)
    acc0 = jnp.zeros((MLA_V, ATT_TQ), F32)
    _, l, acc = lax.fori_loop(0, n_kv, body, (m0, l0, acc0))
    o_ref[...] = (acc / l).T.astype(o_ref.dtype)


def attention(q, k, vt, batch, seq):
    n_tok = q.shape[1]
    nq = seq // ATT_TQ
    n_kv = seq // ATT_TK
    return pl.pallas_call(
        functools.partial(_attn_kernel, n_kv=n_kv),
        grid=(batch, MLA_HEADS, nq),
        in_specs=[
            pl.BlockSpec((None, ATT_TQ, HEAD_PAD), lambda b, h, i: (h, b * nq + i, 0)),
            pl.BlockSpec((None, seq, HEAD_PAD), lambda b, h, i: (h, b, 0)),
            pl.BlockSpec((None, n_kv, MLA_V, ATT_TK), lambda b, h, i: (h, b, 0, 0)),
        ],
        out_specs=pl.BlockSpec((ATT_TQ, MLA_V), lambda b, h, i: (b * nq + i, h)),
        out_shape=jax.ShapeDtypeStruct((n_tok, MLA_HEADS * MLA_V), BF16),
        compiler_params=_params(("parallel", "parallel", "arbitrary")),
        name="mla_attention",
    )(q, k, vt)


def _prep_gla(w_in, w_gate_up, b_gate, g_out, w_out):
    w_in_p = jnp.pad(w_in, ((0, 0), (0, GLA_IN_PAD - GLA_IN_W))).astype(BF16)
    wg = jnp.zeros((2, ROPE_LANES, GLA_KEY_W), F32)
    wg = wg.at[0, 0:GLA_GATE_RANK].set(w_gate_up[0])
    wg = wg.at[1, GLA_GATE_RANK:2 * GLA_GATE_RANK].set(w_gate_up[1])
    return dict(w_in=w_in_p, wg=wg.astype(BF16), bg=b_gate.reshape(2, 1, GLA_KEY_W),
                g_out=g_out.reshape(1, GLA_VAL_W), w_out=w_out.astype(BF16))


def _prep_mla(w_in, g_q, w_uq, g_kv, w_ukv, w_out):
    c0 = MLA_Q_RANK + MLA_KV_RANK
    half = MLA_ROPE // 2
    zpad = jnp.zeros((D_MODEL, ROPE_LANES - MLA_ROPE), F32)
    w_in_ext = jnp.concatenate(
        [w_in, zpad, -w_in[:, c0 + half:c0 + MLA_ROPE], w_in[:, c0:c0 + half], zpad], axis=1).astype(BF16)
    uq = w_uq.reshape(MLA_Q_RANK, MLA_HEADS, MLA_QK)
    rope = uq[:, :, MLA_NOPE:]
    z = jnp.zeros((MLA_Q_RANK, MLA_HEADS, ROPE_LANES - MLA_ROPE), F32)
    wn = uq[:, :, :MLA_NOPE].reshape(MLA_Q_RANK, -1).astype(BF16)
    wr = jnp.concatenate([rope, z], axis=2).reshape(MLA_Q_RANK, -1).astype(BF16)
    wrr = jnp.concatenate([-rope[:, :, half:], rope[:, :, :half], z], axis=2).reshape(MLA_Q_RANK, -1).astype(BF16)
    ukv = w_ukv.reshape(MLA_KV_RANK, MLA_HEADS, MLA_NOPE + MLA_V)
    wk = ukv[:, :, :MLA_NOPE].reshape(MLA_KV_RANK, -1).astype(BF16)
    wvt = ukv[:, :, MLA_NOPE:].reshape(MLA_KV_RANK, -1).T.astype(BF16)
    freqs = ROPE_THETA ** (-jnp.arange(half, dtype=F32) / half)
    freqs = jnp.tile(freqs, ROPE_LANES // half).reshape(1, ROPE_LANES)
    return dict(w_in=w_in_ext, g_q=g_q.reshape(1, -1), g_kv=g_kv.reshape(1, -1), freqs=freqs,
                wn=wn, wr=wr, wrr=wrr, wk=wk, wvt=wvt, w_out=w_out.astype(BF16))


def _prep_ffn(w_up, conv_w, conv_b, w_down):
    def split(m):
        lead = m.shape[:-1]
        a = m[..., :D_FF].reshape(*lead, N_FF_CHUNKS, FF_CHUNK)
        g = m[..., D_FF:].reshape(*lead, N_FF_CHUNKS, FF_CHUNK)
        return jnp.concatenate([a, g], axis=-1)

    wup = jnp.transpose(split(w_up), (1, 0, 2)).astype(BF16)
    taps = jnp.transpose(split(conv_w.reshape(3, 2 * D_FF)), (1, 0, 2))
    bias = split(conv_b.reshape(1, 2 * D_FF)).transpose(1, 0, 2)
    cpar = jnp.concatenate([taps, bias, jnp.zeros((N_FF_CHUNKS, 4, 2 * FF_CHUNK), F32)], axis=1)
    wdown = w_down.reshape(N_FF_CHUNKS, FF_CHUNK, D_MODEL).astype(BF16)
    return dict(wup=wup, cpar=cpar, wdown=wdown)


def _trunk(x3, norm_mix, norm_ffn, norm_final, gla, mla, ffn):
    batch, seq, d = x3.shape
    x = x3.reshape(batch * seq, d)

    proj = norm_matmul(x, norm_mix[0:1], gla["w_in"], "gla_in_proj")
    o_bwd = gla_pass(proj, gla["wg"][1], gla["bg"][1], batch, seq, bwd=True)
    x = gla_pass(proj, gla["wg"][0], gla["bg"][0], batch, seq, bwd=False,
                 extra=(o_bwd, gla["g_out"], gla["w_out"], x))
    x = conv_ffn(x, norm_ffn[0:1], ffn[0]["wup"], ffn[0]["cpar"], ffn[0]["wdown"], seq)

    proj = norm_matmul(x, norm_mix[1:2], mla["w_in"], "mla_in_proj")
    q, k, vt = mla_qkv(proj, mla["g_q"], mla["g_kv"], mla["freqs"], mla["wn"], mla["wr"], mla["wrr"],
                       mla["wk"], mla["wvt"], seq)
    att = attention(q, k, vt, batch, seq)
    x = matmul_residual(x, att, mla["w_out"], "mla_out_proj")
    x = conv_ffn(x, norm_ffn[1:2], ffn[1]["wup"], ffn[1]["cpar"], ffn[1]["wdown"], seq,
                 g_final=norm_final.reshape(1, d))
    return x.reshape(batch, seq, d)


def kernel(x_prompt, x_sample, norm_mix, norm_ffn, norm_final, gla_w_in, gla_w_gate_up, gla_b_gate, gla_g_out, gla_w_out, mla_w_in, mla_g_q, mla_w_uq, mla_g_kv, mla_w_ukv, mla_w_out, ffn_w_up, ffn_conv_w, ffn_conv_b, ffn_w_down):
    gla = _prep_gla(gla_w_in[0], gla_w_gate_up[0], gla_b_gate[0], gla_g_out[0], gla_w_out[0])
    mla = _prep_mla(mla_w_in[0], mla_g_q[0], mla_w_uq[0], mla_g_kv[0], mla_w_ukv[0], mla_w_out[0])
    ffn = [_prep_ffn(ffn_w_up[i], ffn_conv_w[i], ffn_conv_b[i], ffn_w_down[i]) for i in range(2)]
    y_prompt = _trunk(x_prompt, norm_mix, norm_ffn, norm_final, gla, mla, ffn)
    y_sample = _trunk(x_sample, norm_mix, norm_ffn, norm_final, gla, mla, ffn)
    return (y_prompt, y_sample)
```

```python
import functools
import math

import jax
import jax.numpy as jnp
from jax import lax
from jax.experimental import pallas as pl
from jax.experimental.pallas import tpu as pltpu

F32 = jnp.float32
BF16 = jnp.bfloat16

D_MODEL = 1024
EPS = 1e-6

GLA_HEADS = 4
GLA_DK = 128
GLA_DV = 256
GLA_KEY_W = GLA_HEADS * GLA_DK
GLA_VAL_W = GLA_HEADS * GLA_DV
GLA_GATE_RANK = 16
GLA_GATE_TAU = 16.0
GLA_CHUNK = 64
GLA_IN_W = 2 * GLA_KEY_W + 2 * GLA_VAL_W + 2 * GLA_GATE_RANK
GLA_IN_PAD = 3200
GLA_GATE_COL = 2 * GLA_KEY_W + 2 * GLA_VAL_W

MLA_HEADS = 16
MLA_Q_RANK = 384
MLA_KV_RANK = 256
MLA_NOPE = 128
MLA_ROPE = 64
MLA_V = 128
MLA_QK = MLA_NOPE + MLA_ROPE
MLA_IN_W = MLA_Q_RANK + MLA_KV_RANK + MLA_ROPE
MLA_IN_EXT = 896
ROPE_THETA = 10000.0
ROPE_LANES = 128
HEAD_PAD = 256
VT_ROWS = MLA_V + 16

D_FF = 2816
FF_CHUNK = 256
N_FF_CHUNKS = D_FF // FF_CHUNK

SUBLANES = 8
VMEM_LIMIT = 56 * 1024 * 1024

TOK_TILE = 512
GLA_TILE = 256
ATT_TQ = 512
ATT_TK = 512
ATT_SUB = 256
ATT_SUBS = 8


def _rms(x, g):
    return x * lax.rsqrt(jnp.mean(x * x, axis=-1, keepdims=True) + EPS) * g


def _params(sem):
    return pltpu.CompilerParams(dimension_semantics=sem, vmem_limit_bytes=VMEM_LIMIT)


def _norm_matmul_kernel(x_ref, g_ref, w_ref, o_ref):
    hn = _rms(x_ref[...], g_ref[...]).astype(BF16)
    o_ref[...] = jnp.dot(hn, w_ref[...], preferred_element_type=F32).astype(o_ref.dtype)


def norm_matmul(x, g, w, name):
    n_tok, d = x.shape
    n_out = w.shape[1]
    return pl.pallas_call(
        _norm_matmul_kernel,
        grid=(n_tok // TOK_TILE,),
        in_specs=[
            pl.BlockSpec((TOK_TILE, d), lambda i: (i, 0)),
            pl.BlockSpec((1, d), lambda i: (0, 0)),
            pl.BlockSpec((d, n_out), lambda i: (0, 0)),
        ],
        out_specs=pl.BlockSpec((TOK_TILE, n_out), lambda i: (i, 0)),
        out_shape=jax.ShapeDtypeStruct((n_tok, n_out), BF16),
        compiler_params=_params(("parallel",)),
        name=name,
    )(x, g, w)


def _matmul_residual_kernel(x_ref, a_ref, w_ref, o_ref):
    o_ref[...] = x_ref[...] + jnp.dot(a_ref[...], w_ref[...], preferred_element_type=F32)


def matmul_residual(x, a, w, name):
    n_tok, d = x.shape
    k = a.shape[1]
    return pl.pallas_call(
        _matmul_residual_kernel,
        grid=(n_tok // TOK_TILE,),
        in_specs=[
            pl.BlockSpec((TOK_TILE, d), lambda i: (i, 0)),
            pl.BlockSpec((TOK_TILE, k), lambda i: (i, 0)),
            pl.BlockSpec((k, d), lambda i: (0, 0)),
        ],
        out_specs=pl.BlockSpec((TOK_TILE, d), lambda i: (i, 0)),
        out_shape=jax.ShapeDtypeStruct((n_tok, d), F32),
        compiler_params=_params(("parallel",)),
        name=name,
    )(x, a, w)


def _gla_kernel(*refs, bwd, final):
    if final:
        (q_ref, k_ref, v_ref, g_ref, wg_ref, bg_ref, r_ref, ob_ref, gout_ref, wout_ref, x_ref,
         o_ref, state_ref, gated_ref) = refs
    else:
        q_ref, k_ref, v_ref, g_ref, wg_ref, bg_ref, o_ref, state_ref = refs
    t = GLA_TILE
    n_chunks = t // GLA_CHUNK

    @pl.when(pl.program_id(1) == 0)
    def _():
        state_ref[...] = jnp.zeros_like(state_ref)

    z = jnp.dot(g_ref[...], wg_ref[...], preferred_element_type=F32) + bg_ref[...]
    la = -(jnp.maximum(-z, 0.0) + jnp.log1p(jnp.exp(-jnp.abs(z)))) * (1.0 / GLA_GATE_TAU)

    rows = lax.broadcasted_iota(jnp.int32, (t, t), 0)
    cols = lax.broadcasted_iota(jnp.int32, (t, t), 1)
    shift = GLA_CHUNK.bit_length() - 1
    same = (rows >> shift) == (cols >> shift)
    if bwd:
        cum_m = same & (cols >= rows)
        att_m = same & (cols > rows)
    else:
        cum_m = same & (cols <= rows)
        att_m = same & (cols <= rows)
    cum_mat = jnp.where(cum_m, 1.0, 0.0).astype(BF16)
    tot_mat = jnp.where(same, 1.0, 0.0).astype(BF16)

    la_hi = la.astype(BF16)
    la_lo = (la - la_hi.astype(F32)).astype(BF16)
    cum = (jnp.dot(cum_mat, la_hi, preferred_element_type=F32)
           + jnp.dot(cum_mat, la_lo, preferred_element_type=F32))
    tot = (jnp.dot(tot_mat, la_hi, preferred_element_type=F32)
           + jnp.dot(tot_mat, la_lo, preferred_element_type=F32))

    q = q_ref[...].astype(F32) * (GLA_DK ** -0.5)
    k = k_ref[...].astype(F32)
    q_dec = (q * jnp.exp(cum)).astype(BF16)
    k_inv = (k * jnp.exp(-cum)).astype(BF16)
    k_end = (k * jnp.exp(tot - cum)).astype(BF16)
    decay = jnp.exp(tot)

    order = range(n_chunks - 1, -1, -1) if bwd else range(n_chunks)
    for h in range(GLA_HEADS):
        ks = slice(h * GLA_DK, (h + 1) * GLA_DK)
        vs = slice(h * GLA_DV, (h + 1) * GLA_DV)
        qd = q_dec[:, ks]
        vh = v_ref[:, vs]
        sc = lax.dot_general(qd, k_inv[:, ks], (((1,), (1,)), ((), ())), preferred_element_type=F32)
        p = jnp.where(att_m, sc, 0.0).astype(BF16)
        o_intra = jnp.dot(p, vh, preferred_element_type=F32)
        st = state_ref[h]
        for n in order:
            rs = slice(n * GLA_CHUNK, (n + 1) * GLA_CHUNK)
            o_inter = lax.dot_general(qd[rs], st.astype(BF16), (((1,), (1,)), ((), ())),
                                      preferred_element_type=F32)
            kv_t = lax.dot_general(vh[rs], k_end[rs, ks], (((0,), (0,)), ((), ())),
                                   preferred_element_type=F32)
            st = decay[n * GLA_CHUNK:n * GLA_CHUNK + 1, ks] * st + kv_t
            o_n = o_intra[rs] + o_inter
            if final:
                o_n = o_n + ob_ref[rs, vs]
                o_n = o_n * lax.rsqrt(jnp.mean(o_n * o_n, axis=-1, keepdims=True) + EPS) * gout_ref[:, vs]
                r = r_ref[rs, vs].astype(F32)
                o_n = o_n * (r / (1.0 + jnp.exp(-r)))
                gated_ref[rs, vs] = o_n.astype(BF16)
            else:
                o_ref[rs, vs] = o_n
        state_ref[h] = st

    if final:
        o_ref[...] = x_ref[...] + jnp.dot(gated_ref[...], wout_ref[...], preferred_element_type=F32)


def gla_pass(proj, wg, bg, batch, seq, *, bwd, extra=None):
    n_tok = proj.shape[0]
    t = GLA_TILE
    nb = seq // t
    final = extra is not None

    def row(b, j):
        return b * nb + ((nb - 1 - j) if bwd else j)

    def col(c):
        return lambda b, j: (row(b, j), c)

    in_specs = [
        pl.BlockSpec((t, GLA_KEY_W), col(0)),
        pl.BlockSpec((t, GLA_KEY_W), col(1)),
        pl.BlockSpec((t, GLA_VAL_W), col(1)),
        pl.BlockSpec((t, ROPE_LANES), col(GLA_GATE_COL // ROPE_LANES)),
        pl.BlockSpec((ROPE_LANES, GLA_KEY_W), lambda b, j: (0, 0)),
        pl.BlockSpec((1, GLA_KEY_W), lambda b, j: (0, 0)),
    ]
    args = [proj, proj, proj, proj, wg, bg]
    scratch = [pltpu.VMEM((GLA_HEADS, GLA_DV, GLA_DK), F32)]
    if final:
        o_bwd, g_out, w_out, x = extra
        in_specs += [
            pl.BlockSpec((t, GLA_VAL_W), col(2)),
            pl.BlockSpec((t, GLA_VAL_W), col(0)),
            pl.BlockSpec((1, GLA_VAL_W), lambda b, j: (0, 0)),
            pl.BlockSpec((GLA_VAL_W, D_MODEL), lambda b, j: (0, 0)),
            pl.BlockSpec((t, D_MODEL), col(0)),
        ]
        args += [proj, o_bwd, g_out, w_out, x]
        scratch.append(pltpu.VMEM((t, GLA_VAL_W), BF16))
    return pl.pallas_call(
        functools.partial(_gla_kernel, bwd=bwd, final=final),
        grid=(batch, nb),
        in_specs=in_specs,
        out_specs=pl.BlockSpec((t, GLA_VAL_W), col(0)),
        out_shape=jax.ShapeDtypeStruct((n_tok, GLA_VAL_W), F32),
        scratch_shapes=scratch,
        compiler_params=_params(("parallel", "arbitrary")),
        name="gla_fwd_out" if final else "gla_bwd",
    )(*args)


def _ffn_kernel(*refs, tiles_per_seq, final):
    if final:
        (x_ref, xp_ref, xn_ref, g_ref, wup_ref, cpar_ref, wdown_ref, gfin_ref, o_ref,
         hn_ref, hnb_ref, acc_ref) = refs
    else:
        x_ref, xp_ref, xn_ref, g_ref, wup_ref, cpar_ref, wdown_ref, o_ref, hn_ref, hnb_ref, acc_ref = refs
    t = TOK_TILE
    ext = t + 2 * SUBLANES
    i = pl.program_id(0)
    g = g_ref[...]
    keep_prev = jnp.where(i % tiles_per_seq == 0, 0.0, 1.0)
    keep_next = jnp.where((i + 1) % tiles_per_seq == 0, 0.0, 1.0)
    hn_ref[0:SUBLANES, :] = _rms(xp_ref[...], g) * keep_prev
    hn_ref[SUBLANES:SUBLANES + t, :] = _rms(x_ref[...], g)
    hn_ref[SUBLANES + t:ext, :] = _rms(xn_ref[...], g) * keep_next
    hnb_ref[...] = hn_ref[...].astype(BF16)
    acc_ref[...] = x_ref[...]

    def body(c, carry):
        u = jnp.dot(hnb_ref[...], wup_ref[c], preferred_element_type=F32)
        cp = cpar_ref[c]
        conv = (cp[0:1] * pltpu.roll(u, 1, 0) + cp[1:2] * u
                + cp[2:3] * pltpu.roll(u, ext - 1, 0) + cp[3:4])
        conv = conv[SUBLANES:SUBLANES + t]
        a = conv[:, :FF_CHUNK]
        gt = conv[:, FF_CHUNK:]
        act = (a * (gt / (1.0 + jnp.exp(-gt)))).astype(BF16)
        acc_ref[...] += jnp.dot(act, wdown_ref[c], preferred_element_type=F32)
        return carry

    lax.fori_loop(0, N_FF_CHUNKS, body, 0)
    out = acc_ref[...]
    if final:
        out = _rms(out, gfin_ref[...])
    o_ref[...] = out


def conv_ffn(x, g, wup, cpar, wdown, seq, g_final=None):
    n_tok, d = x.shape
    t = TOK_TILE
    r8 = t // SUBLANES
    last8 = n_tok // SUBLANES - 1
    final = g_final is not None
    const3 = lambda i: (0, 0, 0)
    in_specs = [
        pl.BlockSpec((t, d), lambda i: (i, 0)),
        pl.BlockSpec((SUBLANES, d), lambda i: (jnp.maximum(i * r8 - 1, 0), 0)),
        pl.BlockSpec((SUBLANES, d), lambda i: (jnp.minimum((i + 1) * r8, last8), 0)),
        pl.BlockSpec((1, d), lambda i: (0, 0)),
        pl.BlockSpec(wup.shape, const3, pipeline_mode=pl.Buffered(1)),
        pl.BlockSpec(cpar.shape, const3, pipeline_mode=pl.Buffered(1)),
        pl.BlockSpec(wdown.shape, const3, pipeline_mode=pl.Buffered(1)),
    ]
    args = [x, x, x, g, wup, cpar, wdown]
    if final:
        in_specs.append(pl.BlockSpec((1, d), lambda i: (0, 0)))
        args.append(g_final)
    return pl.pallas_call(
        functools.partial(_ffn_kernel, tiles_per_seq=seq // t, final=final),
        grid=(n_tok // t,),
        in_specs=in_specs,
        out_specs=pl.BlockSpec((t, d), lambda i: (i, 0)),
        out_shape=jax.ShapeDtypeStruct((n_tok, d), F32),
        scratch_shapes=[pltpu.VMEM((t + 2 * SUBLANES, d), F32), pltpu.VMEM((t + 2 * SUBLANES, d), BF16),
                        pltpu.VMEM((t, d), F32)],
        compiler_params=_params(("parallel",)),
        name="conv_ffn_final" if final else "conv_ffn",
    )(*args)


def _mla_qkv_kernel(p_ref, gq_ref, gkv_ref, freq_ref, wn_ref, wr_ref, wrr_ref, wk_ref, wvt_ref,
                    q_ref, k_ref, vt_ref, *, tiles_per_seq):
    t = TOK_TILE
    i = pl.program_id(0)
    pos0 = (i % tiles_per_seq) * t
    pos = (pos0 + lax.broadcasted_iota(jnp.int32, (t, ROPE_LANES), 0)).astype(F32)
    ang = pos * freq_ref[...]
    cos = jnp.cos(ang)
    sin = jnp.sin(ang)

    cq = _rms(p_ref[:, 0:MLA_Q_RANK].astype(F32), gq_ref[...]).astype(BF16)
    ckv = _rms(p_ref[:, MLA_Q_RANK:MLA_Q_RANK + MLA_KV_RANK].astype(F32), gkv_ref[...]).astype(BF16)
    c0 = MLA_Q_RANK + MLA_KV_RANK
    kr = (p_ref[:, c0:c0 + ROPE_LANES].astype(F32) * cos
          + p_ref[:, c0 + ROPE_LANES:c0 + 2 * ROPE_LANES].astype(F32) * sin).astype(BF16)

    scale = (MLA_QK ** -0.5) * math.log2(math.e)
    nt_dims = (((1,), (1,)), ((), ()))
    cos_t = cos.T
    sin_t = sin.T
    qn = lax.dot_general(wn_ref[...], cq, nt_dims, preferred_element_type=F32) * scale
    qr = lax.dot_general(wr_ref[...], cq, nt_dims, preferred_element_type=F32)
    qrr = lax.dot_general(wrr_ref[...], cq, nt_dims, preferred_element_type=F32)
    kn = jnp.dot(ckv, wk_ref[...], preferred_element_type=F32)
    vt = lax.dot_general(wvt_ref[...], ckv, nt_dims, preferred_element_type=F32)
    for h in range(MLA_HEADS):
        hs = slice(h * MLA_NOPE, (h + 1) * MLA_NOPE)
        q_ref[h, 0:MLA_NOPE, :] = qn[hs, :].astype(BF16)
        q_ref[h, MLA_NOPE:HEAD_PAD, :] = ((qr[hs, :] * cos_t + qrr[hs, :] * sin_t) * scale).astype(BF16)
        k_ref[h, :, 0:MLA_NOPE] = kn[:, hs].astype(BF16)
        k_ref[h, :, MLA_NOPE:HEAD_PAD] = kr
        vt_ref[h, 0, 0:MLA_V] = vt[hs, :].astype(BF16)
        vt_ref[h, 0, MLA_V:VT_ROWS] = jnp.ones((VT_ROWS - MLA_V, t), BF16)


def mla_qkv(proj, gq, gkv, freqs, wn, wr, wrr, wk, wvt, seq):
    n_tok = proj.shape[0]
    t = TOK_TILE
    nt = n_tok // t
    c2 = lambda i: (0, 0)
    return pl.pallas_call(
        functools.partial(_mla_qkv_kernel, tiles_per_seq=seq // t),
        grid=(nt,),
        in_specs=[
            pl.BlockSpec((t, MLA_IN_EXT), lambda i: (i, 0)),
            pl.BlockSpec(gq.shape, c2),
            pl.BlockSpec(gkv.shape, c2),
            pl.BlockSpec(freqs.shape, c2),
            pl.BlockSpec(wn.shape, c2),
            pl.BlockSpec(wr.shape, c2),
            pl.BlockSpec(wrr.shape, c2),
            pl.BlockSpec(wk.shape, c2),
            pl.BlockSpec(wvt.shape, c2),
        ],
        out_specs=[
            pl.BlockSpec((MLA_HEADS, HEAD_PAD, t), lambda i: (0, 0, i)),
            pl.BlockSpec((MLA_HEADS, t, HEAD_PAD), lambda i: (0, i, 0)),
            pl.BlockSpec((MLA_HEADS, 1, VT_ROWS, t), lambda i: (0, i, 0, 0)),
        ],
        out_shape=[
            jax.ShapeDtypeStruct((MLA_HEADS, HEAD_PAD, n_tok), BF16),
            jax.ShapeDtypeStruct((MLA_HEADS, n_tok, HEAD_PAD), BF16),
            jax.ShapeDtypeStruct((MLA_HEADS, nt, VT_ROWS, t), BF16),
        ],
        compiler_params=_params(("parallel",)),
        name="mla_qkv",
    )(proj, gq, gkv, freqs, wn, wr, wrr, wk, wvt)


def _attn_kernel(q_ref, k_ref, vt_ref, o_ref, sa_ref, sb_ref, *, n_span, subs):
    q_t = q_ref[...]
    span = subs * ATT_SUB

    def qk_block(j, b, dst_ref, cm):
        row0 = pl.multiple_of(j * span + b * ATT_SUB, ATT_SUB)
        s = jnp.dot(k_ref[pl.ds(row0, ATT_SUB), :], q_t, preferred_element_type=F32)
        dst_ref[b * ATT_SUB:(b + 1) * ATT_SUB, :] = s
        c = jnp.max(s, axis=0, keepdims=True)
        return c if cm is None else jnp.maximum(cm, c)

    def pv_block(j, b, src_ref, m_new, pv):
        p = jnp.exp2(src_ref[b * ATT_SUB:(b + 1) * ATT_SUB, :] - m_new).astype(BF16)
        blk, off = divmod(b * ATT_SUB, ATT_TK)
        vt = vt_ref[j * (span // ATT_TK) + blk, :, off:off + ATT_SUB]
        d = jnp.dot(vt, p, preferred_element_type=F32)
        return d if pv is None else pv + d

    def step(j, cur_ref, nxt_ref, cm, m, acc, last=False):
        m_new = jnp.maximum(m, cm)
        alpha = jnp.exp2(m - m_new)
        pv = None
        cm_next = None
        for b in range(subs):
            if not last:
                cm_next = qk_block(j + 1, b, nxt_ref, cm_next)
            pv = pv_block(j, b, cur_ref, m_new, pv)
        return cm_next, m_new, alpha * acc + pv

    def pair(jj, carry):
        cm, m, acc = carry
        cm, m, acc = step(2 * jj, sa_ref, sb_ref, cm, m, acc)
        return step(2 * jj + 1, sb_ref, sa_ref, cm, m, acc)

    cm = None
    for b in range(subs):
        cm = qk_block(0, b, sa_ref, cm)
    m0 = jnp.full((1, ATT_TQ), -jnp.inf, F32)
    acc0 = jnp.zeros((VT_ROWS, ATT_TQ), F32)
    cm, m, acc = lax.fori_loop(0, n_span // 2 - 1, pair, (cm, m0, acc0))
    cm, m, acc = step(n_span - 2, sa_ref, sb_ref, cm, m, acc)
    _, _, acc = step(n_span - 1, sb_ref, sa_ref, cm, m, acc, last=True)
    o_ref[...] = (acc[:MLA_V] / acc[MLA_V:MLA_V + 1]).T.astype(o_ref.dtype)


def attention(q, k, vt, batch, seq):
    n_tok = k.shape[1]
    nq = seq // ATT_TQ
    n_kv = seq // ATT_TK
    subs = ATT_SUBS if seq >= 4 * ATT_SUBS * ATT_SUB else 2
    span = subs * ATT_SUB
    n_span = seq // span
    assert n_span % 2 == 0 and span % ATT_TK == 0
    return pl.pallas_call(
        functools.partial(_attn_kernel, n_span=n_span, subs=subs),
        grid=(batch, MLA_HEADS, nq),
        in_specs=[
            pl.BlockSpec((None, HEAD_PAD, ATT_TQ), lambda b, h, i: (h, 0, b * nq + i)),
            pl.BlockSpec((None, seq, HEAD_PAD), lambda b, h, i: (h, b, 0)),
            pl.BlockSpec((None, n_kv, VT_ROWS, ATT_TK), lambda b, h, i: (h, b, 0, 0)),
        ],
        out_specs=pl.BlockSpec((ATT_TQ, MLA_V), lambda b, h, i: (b * nq + i, h)),
        out_shape=jax.ShapeDtypeStruct((n_tok, MLA_HEADS * MLA_V), BF16),
        scratch_shapes=[pltpu.VMEM((span, ATT_TQ), F32), pltpu.VMEM((span, ATT_TQ), F32)],
        compiler_params=_params(("parallel", "parallel", "arbitrary")),
        name="mla_attention",
    )(q, k, vt)


def _prep_gla(w_in, w_gate_up, b_gate, g_out, w_out):
    w_in_p = jnp.pad(w_in, ((0, 0), (0, GLA_IN_PAD - GLA_IN_W))).astype(BF16)
    wg = jnp.zeros((2, ROPE_LANES, GLA_KEY_W), F32)
    wg = wg.at[0, 0:GLA_GATE_RANK].set(w_gate_up[0])
    wg = wg.at[1, GLA_GATE_RANK:2 * GLA_GATE_RANK].set(w_gate_up[1])
    return dict(w_in=w_in_p, wg=wg.astype(BF16), bg=b_gate.reshape(2, 1, GLA_KEY_W),
                g_out=g_out.reshape(1, GLA_VAL_W), w_out=w_out.astype(BF16))


def _prep_mla(w_in, g_q, w_uq, g_kv, w_ukv, w_out):
    c0 = MLA_Q_RANK + MLA_KV_RANK
    half = MLA_ROPE // 2
    zpad = jnp.zeros((D_MODEL, ROPE_LANES - MLA_ROPE), F32)
    w_in_ext = jnp.concatenate(
        [w_in, zpad, -w_in[:, c0 + half:c0 + MLA_ROPE], w_in[:, c0:c0 + half], zpad], axis=1).astype(BF16)
    uq = w_uq.reshape(MLA_Q_RANK, MLA_HEADS, MLA_QK)
    rope = uq[:, :, MLA_NOPE:]
    z = jnp.zeros((MLA_Q_RANK, MLA_HEADS, ROPE_LANES - MLA_ROPE), F32)
    wn = uq[:, :, :MLA_NOPE].reshape(MLA_Q_RANK, -1).T.astype(BF16)
    wr = jnp.concatenate([rope, z], axis=2).reshape(MLA_Q_RANK, -1).T.astype(BF16)
    wrr = jnp.concatenate([-rope[:, :, half:], rope[:, :, :half], z], axis=2).reshape(MLA_Q_RANK, -1).T.astype(BF16)
    ukv = w_ukv.reshape(MLA_KV_RANK, MLA_HEADS, MLA_NOPE + MLA_V)
    wk = ukv[:, :, :MLA_NOPE].reshape(MLA_KV_RANK, -1).astype(BF16)
    wvt = ukv[:, :, MLA_NOPE:].reshape(MLA_KV_RANK, -1).T.astype(BF16)
    freqs = ROPE_THETA ** (-jnp.arange(half, dtype=F32) / half)
    freqs = jnp.tile(freqs, ROPE_LANES // half).reshape(1, ROPE_LANES)
    return dict(w_in=w_in_ext, g_q=g_q.reshape(1, -1), g_kv=g_kv.reshape(1, -1), freqs=freqs,
                wn=wn, wr=wr, wrr=wrr, wk=wk, wvt=wvt, w_out=w_out.astype(BF16))


def _prep_ffn(w_up, conv_w, conv_b, w_down):
    def split(m):
        lead = m.shape[:-1]
        a = m[..., :D_FF].reshape(*lead, N_FF_CHUNKS, FF_CHUNK)
        g = m[..., D_FF:].reshape(*lead, N_FF_CHUNKS, FF_CHUNK)
        return jnp.concatenate([a, g], axis=-1)

    wup = jnp.transpose(split(w_up), (1, 0, 2)).astype(BF16)
    taps = jnp.transpose(split(conv_w.reshape(3, 2 * D_FF)), (1, 0, 2))
    bias = split(conv_b.reshape(1, 2 * D_FF)).transpose(1, 0, 2)
    cpar = jnp.concatenate([taps, bias, jnp.zeros((N_FF_CHUNKS, 4, 2 * FF_CHUNK), F32)], axis=1)
    wdown = w_down.reshape(N_FF_CHUNKS, FF_CHUNK, D_MODEL).astype(BF16)
    return dict(wup=wup, cpar=cpar, wdown=wdown)


def _trunk(x3, norm_mix, norm_ffn, norm_final, gla, mla, ffn):
    batch, seq, d = x3.shape
    x = x3.reshape(batch * seq, d)

    proj = norm_matmul(x, norm_mix[0:1], gla["w_in"], "gla_in_proj")
    o_bwd = gla_pass(proj, gla["wg"][1], gla["bg"][1], batch, seq, bwd=True)
    x = gla_pass(proj, gla["wg"][0], gla["bg"][0], batch, seq, bwd=False,
                 extra=(o_bwd, gla["g_out"], gla["w_out"], x))
    x = conv_ffn(x, norm_ffn[0:1], ffn[0]["wup"], ffn[0]["cpar"], ffn[0]["wdown"], seq)

    proj = norm_matmul(x, norm_mix[1:2], mla["w_in"], "mla_in_proj")
    q, k, vt = mla_qkv(proj, mla["g_q"], mla["g_kv"], mla["freqs"], mla["wn"], mla["wr"], mla["wrr"],
                       mla["wk"], mla["wvt"], seq)
    att = attention(q, k, vt, batch, seq)
    x = matmul_residual(x, att, mla["w_out"], "mla_out_proj")
    x = conv_ffn(x, norm_ffn[1:2], ffn[1]["wup"], ffn[1]["cpar"], ffn[1]["wdown"], seq,
                 g_final=norm_final.reshape(1, d))
    return x.reshape(batch, seq, d)


def kernel(x_prompt, x_sample, norm_mix, norm_ffn, norm_final, gla_w_in, gla_w_gate_up, gla_b_gate, gla_g_out, gla_w_out, mla_w_in, mla_g_q, mla_w_uq, mla_g_kv, mla_w_ukv, mla_w_out, ffn_w_up, ffn_conv_w, ffn_conv_b, ffn_w_down):
    gla = _prep_gla(gla_w_in[0], gla_w_gate_up[0], gla_b_gate[0], gla_g_out[0], gla_w_out[0])
    mla = _prep_mla(mla_w_in[0], mla_g_q[0], mla_w_uq[0], mla_g_kv[0], mla_w_ukv[0], mla_w_out[0])
    ffn = [_prep_ffn(ffn_w_up[i], ffn_conv_w[i], ffn_conv_b[i], ffn_w_down[i]) for i in range(2)]
    y_prompt = _trunk(x_prompt, norm_mix, norm_ffn, norm_final, gla, mla, ffn)
    y_sample = _trunk(x_sample, norm_mix, norm_ffn, norm_final, gla, mla, ffn)
    return (y_prompt, y_sample)
```

```python
import functools
import math

import jax
import jax.numpy as jnp
from jax import lax
from jax.experimental import pallas as pl
from jax.experimental.pallas import tpu as pltpu

F32 = jnp.float32
BF16 = jnp.bfloat16

D_MODEL = 1024
EPS = 1e-6

GLA_HEADS = 4
GLA_DK = 128
GLA_DV = 256
GLA_KEY_W = GLA_HEADS * GLA_DK
GLA_VAL_W = GLA_HEADS * GLA_DV
GLA_GATE_RANK = 16
GLA_GATE_TAU = 16.0
GLA_CHUNK = 64
GLA_IN_W = 2 * GLA_KEY_W + 2 * GLA_VAL_W + 2 * GLA_GATE_RANK
GLA_IN_PAD = 3200
GLA_GATE_COL = 2 * GLA_KEY_W + 2 * GLA_VAL_W

MLA_HEADS = 16
MLA_Q_RANK = 384
MLA_KV_RANK = 256
MLA_NOPE = 128
MLA_ROPE = 64
MLA_V = 128
MLA_QK = MLA_NOPE + MLA_ROPE
MLA_IN_W = MLA_Q_RANK + MLA_KV_RANK + MLA_ROPE
MLA_IN_EXT = 896
ROPE_THETA = 10000.0
ROPE_LANES = 128
HEAD_PAD = 256
VT_ROWS = MLA_V + 16

D_FF = 2816
FF_CHUNK = 256
N_FF_CHUNKS = D_FF // FF_CHUNK
FF_AHEAD = 3

SUBLANES = 8
VMEM_LIMIT = 56 * 1024 * 1024

TOK_TILE = 512
FFN_TILE = 256
GLA_TILE = 256
ATT_TQ = 512
ATT_TK = 512
ATT_SUB = 256
ATT_SUBS = 8
ATT_NQ = 4


def _rms(x, g):
    return x * lax.rsqrt(jnp.mean(x * x, axis=-1, keepdims=True) + EPS) * g


def _params(sem):
    return pltpu.CompilerParams(dimension_semantics=sem, vmem_limit_bytes=VMEM_LIMIT)


def _norm_matmul_kernel(x_ref, g_ref, w_ref, o_ref):
    hn = _rms(x_ref[...], g_ref[...]).astype(BF16)
    o_ref[...] = jnp.dot(hn, w_ref[...], preferred_element_type=F32).astype(o_ref.dtype)


def norm_matmul(x, g, w, name):
    n_tok, d = x.shape
    n_out = w.shape[1]
    return pl.pallas_call(
        _norm_matmul_kernel,
        grid=(n_tok // TOK_TILE,),
        in_specs=[
            pl.BlockSpec((TOK_TILE, d), lambda i: (i, 0)),
            pl.BlockSpec((1, d), lambda i: (0, 0)),
            pl.BlockSpec((d, n_out), lambda i: (0, 0)),
        ],
        out_specs=pl.BlockSpec((TOK_TILE, n_out), lambda i: (i, 0)),
        out_shape=jax.ShapeDtypeStruct((n_tok, n_out), BF16),
        compiler_params=_params(("parallel",)),
        name=name,
    )(x, g, w)


def _matmul_residual_kernel(x_ref, a_ref, w_ref, o_ref):
    o_ref[...] = x_ref[...] + jnp.dot(a_ref[...], w_ref[...], preferred_element_type=F32)


def matmul_residual(x, a, w, name):
    n_tok, d = x.shape
    k = a.shape[1]
    return pl.pallas_call(
        _matmul_residual_kernel,
        grid=(n_tok // TOK_TILE,),
        in_specs=[
            pl.BlockSpec((TOK_TILE, d), lambda i: (i, 0)),
            pl.BlockSpec((TOK_TILE, k), lambda i: (i, 0)),
            pl.BlockSpec((k, d), lambda i: (0, 0)),
        ],
        out_specs=pl.BlockSpec((TOK_TILE, d), lambda i: (i, 0)),
        out_shape=jax.ShapeDtypeStruct((n_tok, d), F32),
        compiler_params=_params(("parallel",)),
        name=name,
    )(x, a, w)


def _gla_kernel(*refs, bwd, final):
    if final:
        (q_ref, k_ref, v_ref, g_ref, wg_ref, bg_ref, r_ref, ob_ref, gout_ref, wout_ref, x_ref,
         o_ref, state_ref, gated_ref) = refs
    else:
        q_ref, k_ref, v_ref, g_ref, wg_ref, bg_ref, o_ref, state_ref = refs
    t = GLA_TILE
    n_chunks = t // GLA_CHUNK

    @pl.when(pl.program_id(1) == 0)
    def _():
        state_ref[...] = jnp.zeros_like(state_ref)

    z = jnp.dot(g_ref[...], wg_ref[...], preferred_element_type=F32) + bg_ref[...]
    la = -(jnp.maximum(-z, 0.0) + jnp.log(1.0 + jnp.exp(-jnp.abs(z)))) * (1.0 / GLA_GATE_TAU)

    rows = lax.broadcasted_iota(jnp.int32, (t, t), 0)
    cols = lax.broadcasted_iota(jnp.int32, (t, t), 1)
    shift = GLA_CHUNK.bit_length() - 1
    same = (rows >> shift) == (cols >> shift)
    if bwd:
        cum_m = same & (cols >= rows)
        att_m = same & (cols > rows)
    else:
        cum_m = same & (cols <= rows)
        att_m = same & (cols <= rows)
    cum_mat = jnp.where(cum_m, 1.0, 0.0).astype(BF16)
    tot_mat = jnp.where(same, 1.0, 0.0).astype(BF16)

    la_hi = la.astype(BF16)
    la_lo = (la - la_hi.astype(F32)).astype(BF16)
    cum = (jnp.dot(cum_mat, la_hi, preferred_element_type=F32)
           + jnp.dot(cum_mat, la_lo, preferred_element_type=F32))
    tot = (jnp.dot(tot_mat, la_hi, preferred_element_type=F32)
           + jnp.dot(tot_mat, la_lo, preferred_element_type=F32))

    q = q_ref[...].astype(F32) * (GLA_DK ** -0.5)
    k = k_ref[...].astype(F32)
    q_dec = (q * jnp.exp(cum)).astype(BF16)
    k_inv = (k * jnp.exp(-cum)).astype(BF16)
    k_end = (k * jnp.exp(tot - cum)).astype(BF16)
    decay = jnp.exp(tot)

    order = range(n_chunks - 1, -1, -1) if bwd else range(n_chunks)
    heads = range(GLA_HEADS)
    ks = [slice(h * GLA_DK, (h + 1) * GLA_DK) for h in heads]
    vs = [slice(h * GLA_DV, (h + 1) * GLA_DV) for h in heads]
    nt_dims = (((1,), (1,)), ((), ()))
    o_intra = []
    for h in heads:
        sc = lax.dot_general(q_dec[:, ks[h]], k_inv[:, ks[h]], nt_dims, preferred_element_type=F32)
        p = jnp.where(att_m, sc, 0.0).astype(BF16)
        o_intra.append(jnp.dot(p, v_ref[:, vs[h]], preferred_element_type=F32))
    st = [state_ref[h] for h in heads]
    for n in order:
        rs = slice(n * GLA_CHUNK, (n + 1) * GLA_CHUNK)
        for h in heads:
            o_inter = lax.dot_general(q_dec[rs, ks[h]], st[h].astype(BF16), nt_dims,
                                      preferred_element_type=F32)
            kv_t = lax.dot_general(v_ref[rs, vs[h]], k_end[rs, ks[h]], (((0,), (0,)), ((), ())),
                                   preferred_element_type=F32)
            st[h] = decay[n * GLA_CHUNK:n * GLA_CHUNK + 1, ks[h]] * st[h] + kv_t
            o_n = o_intra[h][rs] + o_inter
            if final:
                o_n = o_n + ob_ref[rs, vs[h]]
                o_n = o_n * lax.rsqrt(jnp.mean(o_n * o_n, axis=-1, keepdims=True) + EPS) * gout_ref[:, vs[h]]
                r = r_ref[rs, vs[h]].astype(F32)
                o_n = o_n * (r / (1.0 + jnp.exp(-r)))
                gated_ref[rs, vs[h]] = o_n.astype(BF16)
            else:
                o_ref[rs, vs[h]] = o_n
    for h in heads:
        state_ref[h] = st[h]

    if final:
        o_ref[...] = x_ref[...] + jnp.dot(gated_ref[...], wout_ref[...], preferred_element_type=F32)


def gla_pass(proj, wg, bg, batch, seq, *, bwd, extra=None):
    n_tok = proj.shape[0]
    t = GLA_TILE
    nb = seq // t
    final = extra is not None

    def row(b, j):
        return b * nb + ((nb - 1 - j) if bwd else j)

    def col(c):
        return lambda b, j: (row(b, j), c)

    in_specs = [
        pl.BlockSpec((t, GLA_KEY_W), col(0)),
        pl.BlockSpec((t, GLA_KEY_W), col(1)),
        pl.BlockSpec((t, GLA_VAL_W), col(1)),
        pl.BlockSpec((t, ROPE_LANES), col(GLA_GATE_COL // ROPE_LANES)),
        pl.BlockSpec((ROPE_LANES, GLA_KEY_W), lambda b, j: (0, 0)),
        pl.BlockSpec((1, GLA_KEY_W), lambda b, j: (0, 0)),
    ]
    args = [proj, proj, proj, proj, wg, bg]
    scratch = [pltpu.VMEM((GLA_HEADS, GLA_DV, GLA_DK), F32)]
    if final:
        o_bwd, g_out, w_out, x = extra
        in_specs += [
            pl.BlockSpec((t, GLA_VAL_W), col(2)),
            pl.BlockSpec((t, GLA_VAL_W), col(0)),
            pl.BlockSpec((1, GLA_VAL_W), lambda b, j: (0, 0)),
            pl.BlockSpec((GLA_VAL_W, D_MODEL), lambda b, j: (0, 0)),
            pl.BlockSpec((t, D_MODEL), col(0)),
        ]
        args += [proj, o_bwd, g_out, w_out, x]
        scratch.append(pltpu.VMEM((t, GLA_VAL_W), BF16))
    return pl.pallas_call(
        functools.partial(_gla_kernel, bwd=bwd, final=final),
        grid=(batch, nb),
        in_specs=in_specs,
        out_specs=pl.BlockSpec((t, GLA_VAL_W), col(0)),
        out_shape=jax.ShapeDtypeStruct((n_tok, GLA_VAL_W), F32),
        scratch_shapes=scratch,
        compiler_params=_params(("parallel", "arbitrary")),
        name="gla_fwd_out" if final else "gla_bwd",
    )(*args)


def _ffn_kernel(*refs, tiles_per_seq, final):
    if final:
        (x_ref, xp_ref, xn_ref, g_ref, wup_ref, cpar_ref, wdown_ref, gfin_ref, o_ref,
         hn_ref, hnb_ref, acc_ref) = refs
    else:
        x_ref, xp_ref, xn_ref, g_ref, wup_ref, cpar_ref, wdown_ref, o_ref, hn_ref, hnb_ref, acc_ref = refs
    t = FFN_TILE
    ext = t + 2 * SUBLANES
    i = pl.program_id(0)
    g = g_ref[...]
    keep_prev = jnp.where(i % tiles_per_seq == 0, 0.0, 1.0)
    keep_next = jnp.where((i + 1) % tiles_per_seq == 0, 0.0, 1.0)
    hn_ref[0:SUBLANES, :] = _rms(xp_ref[...], g) * keep_prev
    hn_ref[SUBLANES:SUBLANES + t, :] = _rms(x_ref[...], g)
    hn_ref[SUBLANES + t:ext, :] = _rms(xn_ref[...], g) * keep_next
    hnb_ref[...] = hn_ref[...].astype(BF16)
    acc_ref[...] = x_ref[...]

    def up(c):
        return jnp.dot(hnb_ref[...], wup_ref[c], preferred_element_type=F32)

    ahead = [up(c) for c in range(FF_AHEAD)]
    for c in range(N_FF_CHUNKS):
        u = ahead.pop(0)
        if c + FF_AHEAD < N_FF_CHUNKS:
            ahead.append(up(c + FF_AHEAD))
        cp = cpar_ref[c]
        conv = (cp[0:1] * pltpu.roll(u, 1, 0) + cp[1:2] * u
                + cp[2:3] * pltpu.roll(u, ext - 1, 0) + cp[3:4])
        conv = conv[SUBLANES:SUBLANES + t]
        a = conv[:, :FF_CHUNK]
        gt = conv[:, FF_CHUNK:]
        act = (a * (gt / (1.0 + jnp.exp(-gt)))).astype(BF16)
        acc_ref[...] += jnp.dot(act, wdown_ref[c], preferred_element_type=F32)
    out = acc_ref[...]
    if final:
        out = _rms(out, gfin_ref[...])
    o_ref[...] = out


def conv_ffn(x, g, wup, cpar, wdown, seq, g_final=None):
    n_tok, d = x.shape
    t = FFN_TILE
    r8 = t // SUBLANES
    last8 = n_tok // SUBLANES - 1
    final = g_final is not None
    const3 = lambda i: (0, 0, 0)
    in_specs = [
        pl.BlockSpec((t, d), lambda i: (i, 0)),
        pl.BlockSpec((SUBLANES, d), lambda i: (jnp.maximum(i * r8 - 1, 0), 0)),
        pl.BlockSpec((SUBLANES, d), lambda i: (jnp.minimum((i + 1) * r8, last8), 0)),
        pl.BlockSpec((1, d), lambda i: (0, 0)),
        pl.BlockSpec(wup.shape, const3, pipeline_mode=pl.Buffered(1)),
        pl.BlockSpec(cpar.shape, const3, pipeline_mode=pl.Buffered(1)),
        pl.BlockSpec(wdown.shape, const3, pipeline_mode=pl.Buffered(1)),
    ]
    args = [x, x, x, g, wup, cpar, wdown]
    if final:
        in_specs.append(pl.BlockSpec((1, d), lambda i: (0, 0)))
        args.append(g_final)
    return pl.pallas_call(
        functools.partial(_ffn_kernel, tiles_per_seq=seq // t, final=final),
        grid=(n_tok // t,),
        in_specs=in_specs,
        out_specs=pl.BlockSpec((t, d), lambda i: (i, 0)),
        out_shape=jax.ShapeDtypeStruct((n_tok, d), F32),
        scratch_shapes=[pltpu.VMEM((t + 2 * SUBLANES, d), F32), pltpu.VMEM((t + 2 * SUBLANES, d), BF16),
                        pltpu.VMEM((t, d), F32)],
        compiler_params=_params(("parallel",)),
        name="conv_ffn_final" if final else "conv_ffn",
    )(*args)


def _mla_qkv_kernel(p_ref, gq_ref, gkv_ref, freq_ref, wn_ref, wr_ref, wrr_ref, wk_ref, wvt_ref,
                    q_ref, k_ref, vt_ref, *, tiles_per_seq):
    t = TOK_TILE
    i = pl.program_id(0)
    pos0 = (i % tiles_per_seq) * t
    pos = (pos0 + lax.broadcasted_iota(jnp.int32, (t, ROPE_LANES), 0)).astype(F32)
    ang = pos * freq_ref[...]
    cos = jnp.cos(ang)
    sin = jnp.sin(ang)

    cq = _rms(p_ref[:, 0:MLA_Q_RANK].astype(F32), gq_ref[...]).astype(BF16)
    ckv = _rms(p_ref[:, MLA_Q_RANK:MLA_Q_RANK + MLA_KV_RANK].astype(F32), gkv_ref[...]).astype(BF16)
    c0 = MLA_Q_RANK + MLA_KV_RANK
    kr = (p_ref[:, c0:c0 + ROPE_LANES].astype(F32) * cos
          + p_ref[:, c0 + ROPE_LANES:c0 + 2 * ROPE_LANES].astype(F32) * sin).astype(BF16)

    scale = (MLA_QK ** -0.5) * math.log2(math.e)
    nt_dims = (((1,), (1,)), ((), ()))
    cos_t = cos.T
    sin_t = sin.T
    qn = lax.dot_general(wn_ref[...], cq, nt_dims, preferred_element_type=F32) * scale
    qr = lax.dot_general(wr_ref[...], cq, nt_dims, preferred_element_type=F32)
    qrr = lax.dot_general(wrr_ref[...], cq, nt_dims, preferred_element_type=F32)
    kn = jnp.dot(ckv, wk_ref[...], preferred_element_type=F32)
    vt = lax.dot_general(wvt_ref[...], ckv, nt_dims, preferred_element_type=F32)
    for h in range(MLA_HEADS):
        hs = slice(h * MLA_NOPE, (h + 1) * MLA_NOPE)
        q_ref[h, 0, 0:MLA_NOPE, :] = qn[hs, :].astype(BF16)
        q_ref[h, 0, MLA_NOPE:HEAD_PAD, :] = ((qr[hs, :] * cos_t + qrr[hs, :] * sin_t) * scale).astype(BF16)
        k_ref[h, :, 0:MLA_NOPE] = kn[:, hs].astype(BF16)
        k_ref[h, :, MLA_NOPE:HEAD_PAD] = kr
        vt_ref[h, 0, 0:MLA_V] = vt[hs, :].astype(BF16)
        vt_ref[h, 0, MLA_V:VT_ROWS] = jnp.ones((VT_ROWS - MLA_V, t), BF16)


def mla_qkv(proj, gq, gkv, freqs, wn, wr, wrr, wk, wvt, seq):
    n_tok = proj.shape[0]
    t = TOK_TILE
    nt = n_tok // t
    c2 = lambda i: (0, 0)
    return pl.pallas_call(
        functools.partial(_mla_qkv_kernel, tiles_per_seq=seq // t),
        grid=(nt,),
        in_specs=[
            pl.BlockSpec((t, MLA_IN_EXT), lambda i: (i, 0)),
            pl.BlockSpec(gq.shape, c2),
            pl.BlockSpec(gkv.shape, c2),
            pl.BlockSpec(freqs.shape, c2),
            pl.BlockSpec(wn.shape, c2),
            pl.BlockSpec(wr.shape, c2),
            pl.BlockSpec(wrr.shape, c2),
            pl.BlockSpec(wk.shape, c2),
            pl.BlockSpec(wvt.shape, c2),
        ],
        out_specs=[
            pl.BlockSpec((MLA_HEADS, 1, HEAD_PAD, t), lambda i: (0, i, 0, 0)),
            pl.BlockSpec((MLA_HEADS, t, HEAD_PAD), lambda i: (0, i, 0)),
            pl.BlockSpec((MLA_HEADS, 1, VT_ROWS, t), lambda i: (0, i, 0, 0)),
        ],
        out_shape=[
            jax.ShapeDtypeStruct((MLA_HEADS, nt, HEAD_PAD, t), BF16),
            jax.ShapeDtypeStruct((MLA_HEADS, n_tok, HEAD_PAD), BF16),
            jax.ShapeDtypeStruct((MLA_HEADS, nt, VT_ROWS, t), BF16),
        ],
        compiler_params=_params(("parallel",)),
        name="mla_qkv",
    )(proj, gq, gkv, freqs, wn, wr, wrr, wk, wvt)


def _attn_kernel(q_ref, k_ref, vt_ref, o_ref, sa_ref, sb_ref, *, n_span, subs):
    span = subs * ATT_SUB

    def qk_block(qi, j, b, dst_ref, cm):
        row0 = pl.multiple_of(j * span + b * ATT_SUB, ATT_SUB)
        s = jnp.dot(k_ref[pl.ds(row0, ATT_SUB), :], q_ref[qi], preferred_element_type=F32)
        dst_ref[b * ATT_SUB:(b + 1) * ATT_SUB, :] = s
        c = jnp.max(s, axis=0, keepdims=True)
        return c if cm is None else jnp.maximum(cm, c)

    def pv_block(j, b, src_ref, m_new, pv):
        p = jnp.exp2(src_ref[b * ATT_SUB:(b + 1) * ATT_SUB, :] - m_new).astype(BF16)
        blk, off = divmod(b * ATT_SUB, ATT_TK)
        vt = vt_ref[j * (span // ATT_TK) + blk, :, off:off + ATT_SUB]
        d = jnp.dot(vt, p, preferred_element_type=F32)
        return d if pv is None else pv + d

    def step(j, cur_ref, nxt_ref, q_next, j_next, cm, m, acc):
        m_new = jnp.maximum(m, cm)
        alpha = jnp.exp2(m - m_new)
        pv = None
        cm_next = None
        for b in range(subs):
            cm_next = qk_block(q_next, j_next, b, nxt_ref, cm_next)
            pv = pv_block(j, b, cur_ref, m_new, pv)
        return cm_next, m_new, alpha * acc + pv

    def q_tile(qi, cm):
        def pair(jj, carry):
            cm, m, acc = carry
            cm, m, acc = step(2 * jj, sa_ref, sb_ref, qi, 2 * jj + 1, cm, m, acc)
            return step(2 * jj + 1, sb_ref, sa_ref, qi, 2 * jj + 2, cm, m, acc)

        m0 = jnp.full((1, ATT_TQ), -jnp.inf, F32)
        acc0 = jnp.zeros((VT_ROWS, ATT_TQ), F32)
        cm, m, acc = lax.fori_loop(0, n_span // 2 - 1, pair, (cm, m0, acc0))
        cm, m, acc = step(n_span - 2, sa_ref, sb_ref, qi, n_span - 1, cm, m, acc)
        cm, _, acc = step(n_span - 1, sb_ref, sa_ref, jnp.minimum(qi + 1, ATT_NQ - 1), 0, cm, m, acc)
        row0 = pl.multiple_of(qi * ATT_TQ, ATT_TQ)
        o_ref[pl.ds(row0, ATT_TQ), :] = (acc[:MLA_V] / acc[MLA_V:MLA_V + 1]).T.astype(o_ref.dtype)
        return cm

    cm = None
    for b in range(subs):
        cm = qk_block(0, 0, b, sa_ref, cm)
    lax.fori_loop(0, ATT_NQ, q_tile, cm)


def attention(q, k, vt, batch, seq):
    n_tok = k.shape[1]
    nq = seq // (ATT_NQ * ATT_TQ)
    n_kv = seq // ATT_TK
    subs = ATT_SUBS if seq >= 4 * ATT_SUBS * ATT_SUB else 2
    span = subs * ATT_SUB
    n_span = seq // span
    assert n_span % 2 == 0 and span % ATT_TK == 0
    return pl.pallas_call(
        functools.partial(_attn_kernel, n_span=n_span, subs=subs),
        grid=(batch, MLA_HEADS, nq),
        in_specs=[
            pl.BlockSpec((None, ATT_NQ, HEAD_PAD, ATT_TQ), lambda b, h, i: (h, b * nq + i, 0, 0)),
            pl.BlockSpec((None, seq, HEAD_PAD), lambda b, h, i: (h, b, 0)),
            pl.BlockSpec((None, n_kv, VT_ROWS, ATT_TK), lambda b, h, i: (h, b, 0, 0)),
        ],
        out_specs=pl.BlockSpec((ATT_NQ * ATT_TQ, MLA_V), lambda b, h, i: (b * nq + i, h)),
        out_shape=jax.ShapeDtypeStruct((n_tok, MLA_HEADS * MLA_V), BF16),
        scratch_shapes=[pltpu.VMEM((span, ATT_TQ), F32), pltpu.VMEM((span, ATT_TQ), F32)],
        compiler_params=_params(("parallel", "parallel", "arbitrary")),
        name="mla_attention",
    )(q, k, vt)


def _prep_gla(w_in, w_gate_up, b_gate, g_out, w_out):
    w_in_p = jnp.pad(w_in, ((0, 0), (0, GLA_IN_PAD - GLA_IN_W))).astype(BF16)
    wg = jnp.zeros((2, ROPE_LANES, GLA_KEY_W), F32)
    wg = wg.at[0, 0:GLA_GATE_RANK].set(w_gate_up[0])
    wg = wg.at[1, GLA_GATE_RANK:2 * GLA_GATE_RANK].set(w_gate_up[1])
    return dict(w_in=w_in_p, wg=wg.astype(BF16), bg=b_gate.reshape(2, 1, GLA_KEY_W),
                g_out=g_out.reshape(1, GLA_VAL_W), w_out=w_out.astype(BF16))


def _prep_mla(w_in, g_q, w_uq, g_kv, w_ukv, w_out):
    c0 = MLA_Q_RANK + MLA_KV_RANK
    half = MLA_ROPE // 2
    zpad = jnp.zeros((D_MODEL, ROPE_LANES - MLA_ROPE), F32)
    w_in_ext = jnp.concatenate(
        [w_in, zpad, -w_in[:, c0 + half:c0 + MLA_ROPE], w_in[:, c0:c0 + half], zpad], axis=1).astype(BF16)
    uq = w_uq.reshape(MLA_Q_RANK, MLA_HEADS, MLA_QK)
    rope = uq[:, :, MLA_NOPE:]
    z = jnp.zeros((MLA_Q_RANK, MLA_HEADS, ROPE_LANES - MLA_ROPE), F32)
    wn = uq[:, :, :MLA_NOPE].reshape(MLA_Q_RANK, -1).T.astype(BF16)
    wr = jnp.concatenate([rope, z], axis=2).reshape(MLA_Q_RANK, -1).T.astype(BF16)
    wrr = jnp.concatenate([-rope[:, :, half:], rope[:, :, :half], z], axis=2).reshape(MLA_Q_RANK, -1).T.astype(BF16)
    ukv = w_ukv.reshape(MLA_KV_RANK, MLA_HEADS, MLA_NOPE + MLA_V)
    wk = ukv[:, :, :MLA_NOPE].reshape(MLA_KV_RANK, -1).astype(BF16)
    wvt = ukv[:, :, MLA_NOPE:].reshape(MLA_KV_RANK, -1).T.astype(BF16)
    freqs = ROPE_THETA ** (-jnp.arange(half, dtype=F32) / half)
    freqs = jnp.tile(freqs, ROPE_LANES // half).reshape(1, ROPE_LANES)
    return dict(w_in=w_in_ext, g_q=g_q.reshape(1, -1), g_kv=g_kv.reshape(1, -1), freqs=freqs,
                wn=wn, wr=wr, wrr=wrr, wk=wk, wvt=wvt, w_out=w_out.astype(BF16))


def _prep_ffn(w_up, conv_w, conv_b, w_down):
    def split(m):
        lead = m.shape[:-1]
        a = m[..., :D_FF].reshape(*lead, N_FF_CHUNKS, FF_CHUNK)
        g = m[..., D_FF:].reshape(*lead, N_FF_CHUNKS, FF_CHUNK)
        return jnp.concatenate([a, g], axis=-1)

    wup = jnp.transpose(split(w_up), (1, 0, 2)).astype(BF16)
    taps = jnp.transpose(split(conv_w.reshape(3, 2 * D_FF)), (1, 0, 2))
    bias = split(conv_b.reshape(1, 2 * D_FF)).transpose(1, 0, 2)
    cpar = jnp.concatenate([taps, bias, jnp.zeros((N_FF_CHUNKS, 4, 2 * FF_CHUNK), F32)], axis=1)
    wdown = w_down.reshape(N_FF_CHUNKS, FF_CHUNK, D_MODEL).astype(BF16)
    return dict(wup=wup, cpar=cpar, wdown=wdown)


def _trunk(x3, norm_mix, norm_ffn, norm_final, gla, mla, ffn):
    batch, seq, d = x3.shape
    x = x3.reshape(batch * seq, d)

    proj = norm_matmul(x, norm_mix[0:1], gla["w_in"], "gla_in_proj")
    o_bwd = gla_pass(proj, gla["wg"][1], gla["bg"][1], batch, seq, bwd=True)
    x = gla_pass(proj, gla["wg"][0], gla["bg"][0], batch, seq, bwd=False,
                 extra=(o_bwd, gla["g_out"], gla["w_out"], x))
    x = conv_ffn(x, norm_ffn[0:1], ffn[0]["wup"], ffn[0]["cpar"], ffn[0]["wdown"], seq)

    proj = norm_matmul(x, norm_mix[1:2], mla["w_in"], "mla_in_proj")
    q, k, vt = mla_qkv(proj, mla["g_q"], mla["g_kv"], mla["freqs"], mla["wn"], mla["wr"], mla["wrr"],
                       mla["wk"], mla["wvt"], seq)
    att = attention(q, k, vt, batch, seq)
    x = matmul_residual(x, att, mla["w_out"], "mla_out_proj")
    x = conv_ffn(x, norm_ffn[1:2], ffn[1]["wup"], ffn[1]["cpar"], ffn[1]["wdown"], seq,
                 g_final=norm_final.reshape(1, d))
    return x.reshape(batch, seq, d)


def kernel(x_prompt, x_sample, norm_mix, norm_ffn, norm_final, gla_w_in, gla_w_gate_up, gla_b_gate, gla_g_out, gla_w_out, mla_w_in, mla_g_q, mla_w_uq, mla_g_kv, mla_w_ukv, mla_w_out, ffn_w_up, ffn_conv_w, ffn_conv_b, ffn_w_down):
    gla = _prep_gla(gla_w_in[0], gla_w_gate_up[0], gla_b_gate[0], gla_g_out[0], gla_w_out[0])
    mla = _prep_mla(mla_w_in[0], mla_g_q[0], mla_w_uq[0], mla_g_kv[0], mla_w_ukv[0], mla_w_out[0])
    ffn = [_prep_ffn(ffn_w_up[i], ffn_conv_w[i], ffn_conv_b[i], ffn_w_down[i]) for i in range(2)]
    y_prompt = _trunk(x_prompt, norm_mix, norm_ffn, norm_final, gla, mla, ffn)
    y_sample = _trunk(x_sample, norm_mix, norm_ffn, norm_final, gla, mla, ffn)
    return (y_prompt, y_sample)
```

```python
import functools
import math

import jax
import jax.numpy as jnp
from jax import lax
from jax.experimental import pallas as pl
from jax.experimental.pallas import tpu as pltpu

F32 = jnp.float32
BF16 = jnp.bfloat16

D_MODEL = 1024
EPS = 1e-6

GLA_HEADS = 4
GLA_DK = 128
GLA_DV = 256
GLA_KEY_W = GLA_HEADS * GLA_DK
GLA_VAL_W = GLA_HEADS * GLA_DV
GLA_GATE_RANK = 16
GLA_GATE_TAU = 16.0
GLA_CHUNK = 64
GLA_IN_W = 2 * GLA_KEY_W + 2 * GLA_VAL_W + 2 * GLA_GATE_RANK
GLA_IN_PAD = 3200
GLA_GATE_COL = 2 * GLA_KEY_W + 2 * GLA_VAL_W

MLA_HEADS = 16
MLA_Q_RANK = 384
MLA_KV_RANK = 256
MLA_NOPE = 128
MLA_ROPE = 64
MLA_V = 128
MLA_QK = MLA_NOPE + MLA_ROPE
MLA_IN_W = MLA_Q_RANK + MLA_KV_RANK + MLA_ROPE
MLA_IN_EXT = 896
ROPE_THETA = 10000.0
ROPE_LANES = 128
HEAD_PAD = 256
VT_ROWS = MLA_V + 16

D_FF = 2816
FF_CHUNK = 256
N_FF_CHUNKS = D_FF // FF_CHUNK
FF_AHEAD = 3

SUBLANES = 8
VMEM_LIMIT = 56 * 1024 * 1024

TOK_TILE = 512
FFN_TILE = 256
GLA_TILE = 512
GLA_BLOCK = 256
ATT_TQ = 512
ATT_TK = 512
ATT_SUB = 256
ATT_SUBS = 8
ATT_NQ = 8


def _rms(x, g):
    return x * lax.rsqrt(jnp.mean(x * x, axis=-1, keepdims=True) + EPS) * g


def _params(sem):
    return pltpu.CompilerParams(dimension_semantics=sem, vmem_limit_bytes=VMEM_LIMIT)


def _norm_matmul_kernel(x_ref, g_ref, w_ref, o_ref):
    hn = _rms(x_ref[...], g_ref[...]).astype(BF16)
    o_ref[...] = jnp.dot(hn, w_ref[...], preferred_element_type=F32).astype(o_ref.dtype)


def norm_matmul(x, g, w, name):
    n_tok, d = x.shape
    n_out = w.shape[1]
    return pl.pallas_call(
        _norm_matmul_kernel,
        grid=(n_tok // TOK_TILE,),
        in_specs=[
            pl.BlockSpec((TOK_TILE, d), lambda i: (i, 0)),
            pl.BlockSpec((1, d), lambda i: (0, 0)),
            pl.BlockSpec((d, n_out), lambda i: (0, 0)),
        ],
        out_specs=pl.BlockSpec((TOK_TILE, n_out), lambda i: (i, 0)),
        out_shape=jax.ShapeDtypeStruct((n_tok, n_out), BF16),
        compiler_params=_params(("parallel",)),
        name=name,
    )(x, g, w)


def _matmul_residual_kernel(x_ref, a_ref, w_ref, o_ref):
    o_ref[...] = x_ref[...] + jnp.dot(a_ref[...], w_ref[...], preferred_element_type=F32)


def matmul_residual(x, a, w, name):
    n_tok, d = x.shape
    k = a.shape[1]
    return pl.pallas_call(
        _matmul_residual_kernel,
        grid=(n_tok // TOK_TILE,),
        in_specs=[
            pl.BlockSpec((TOK_TILE, d), lambda i: (i, 0)),
            pl.BlockSpec((TOK_TILE, k), lambda i: (i, 0)),
            pl.BlockSpec((k, d), lambda i: (0, 0)),
        ],
        out_specs=pl.BlockSpec((TOK_TILE, d), lambda i: (i, 0)),
        out_shape=jax.ShapeDtypeStruct((n_tok, d), F32),
        compiler_params=_params(("parallel",)),
        name=name,
    )(x, a, w)


def _gla_kernel(*refs, bwd, final):
    if final:
        (q_ref, k_ref, v_ref, g_ref, wg_ref, bg_ref, r_ref, ob_ref, gout_ref, wout_ref, x_ref,
         o_ref, state_ref, gated_ref) = refs
    else:
        q_ref, k_ref, v_ref, g_ref, wg_ref, bg_ref, o_ref, state_ref = refs
    t = GLA_BLOCK
    n_chunks = t // GLA_CHUNK
    blocks = range(GLA_TILE // t)

    @pl.when(pl.program_id(1) == 0)
    def _():
        state_ref[...] = jnp.zeros_like(state_ref)

    rows = lax.broadcasted_iota(jnp.int32, (t, t), 0)
    cols = lax.broadcasted_iota(jnp.int32, (t, t), 1)
    shift = GLA_CHUNK.bit_length() - 1
    same = (rows >> shift) == (cols >> shift)
    if bwd:
        cum_m = same & (cols >= rows)
        att_m = same & (cols > rows)
    else:
        cum_m = same & (cols <= rows)
        att_m = same & (cols <= rows)
    cum_mat = jnp.where(cum_m, 1.0, 0.0).astype(BF16)
    tot_mat = jnp.where(same, 1.0, 0.0).astype(BF16)

    blk_rows = [slice(i * t, (i + 1) * t) for i in blocks]
    z = [jnp.dot(g_ref[rs, :], wg_ref[...], preferred_element_type=F32) + bg_ref[...] for rs in blk_rows]
    la = [-(jnp.maximum(-zi, 0.0) + jnp.log(1.0 + jnp.exp(-jnp.abs(zi)))) * (1.0 / GLA_GATE_TAU) for zi in z]
    la_hi = [x.astype(BF16) for x in la]
    la_lo = [(x - h.astype(F32)).astype(BF16) for x, h in zip(la, la_hi)]
    cum = [jnp.dot(cum_mat, h, preferred_element_type=F32) + jnp.dot(cum_mat, l, preferred_element_type=F32)
           for h, l in zip(la_hi, la_lo)]
    tot = [jnp.dot(tot_mat, h, preferred_element_type=F32) + jnp.dot(tot_mat, l, preferred_element_type=F32)
           for h, l in zip(la_hi, la_lo)]

    q = [q_ref[rs, :].astype(F32) * (GLA_DK ** -0.5) for rs in blk_rows]
    k = [k_ref[rs, :].astype(F32) for rs in blk_rows]
    q_dec = [(q[i] * jnp.exp(cum[i])).astype(BF16) for i in blocks]
    k_inv = [(k[i] * jnp.exp(-cum[i])).astype(BF16) for i in blocks]
    k_end = [(k[i] * jnp.exp(tot[i] - cum[i])).astype(BF16) for i in blocks]
    decay = [jnp.exp(tot[i]) for i in blocks]

    heads = range(GLA_HEADS)
    ks = [slice(h * GLA_DK, (h + 1) * GLA_DK) for h in heads]
    vs = [slice(h * GLA_DV, (h + 1) * GLA_DV) for h in heads]
    nt_dims = (((1,), (1,)), ((), ()))
    o_intra = [[None] * GLA_HEADS for _ in blocks]
    for h in heads:
        for i in blocks:
            sc = lax.dot_general(q_dec[i][:, ks[h]], k_inv[i][:, ks[h]], nt_dims, preferred_element_type=F32)
            p = jnp.where(att_m, sc, 0.0).astype(BF16)
            o_intra[i][h] = jnp.dot(p, v_ref[blk_rows[i], vs[h]], preferred_element_type=F32)

    st = [state_ref[h] for h in heads]
    for i in (reversed(blocks) if bwd else blocks):
        for n in (range(n_chunks - 1, -1, -1) if bwd else range(n_chunks)):
            cs = slice(n * GLA_CHUNK, (n + 1) * GLA_CHUNK)
            rs = slice(i * t + n * GLA_CHUNK, i * t + (n + 1) * GLA_CHUNK)
            for h in heads:
                o_inter = lax.dot_general(q_dec[i][cs, ks[h]], st[h].astype(BF16), nt_dims,
                                          preferred_element_type=F32)
                kv_t = lax.dot_general(v_ref[rs, vs[h]], k_end[i][cs, ks[h]], (((0,), (0,)), ((), ())),
                                       preferred_element_type=F32)
                st[h] = decay[i][n * GLA_CHUNK:n * GLA_CHUNK + 1, ks[h]] * st[h] + kv_t
                o_n = o_intra[i][h][cs] + o_inter
                if final:
                    o_n = o_n + ob_ref[rs, vs[h]]
                    o_n = o_n * lax.rsqrt(jnp.mean(o_n * o_n, axis=-1, keepdims=True) + EPS) * gout_ref[:, vs[h]]
                    r = r_ref[rs, vs[h]].astype(F32)
                    o_n = o_n * (r / (1.0 + jnp.exp(-r)))
                    gated_ref[rs, vs[h]] = o_n.astype(BF16)
                else:
                    o_ref[rs, vs[h]] = o_n
    for h in heads:
        state_ref[h] = st[h]

    if final:
        o_ref[...] = x_ref[...] + jnp.dot(gated_ref[...], wout_ref[...], preferred_element_type=F32)


def gla_pass(proj, wg, bg, batch, seq, *, bwd, extra=None):
    n_tok = proj.shape[0]
    t = GLA_TILE
    nb = seq // t
    final = extra is not None

    def row(b, j):
        return b * nb + ((nb - 1 - j) if bwd else j)

    def col(c):
        return lambda b, j: (row(b, j), c)

    in_specs = [
        pl.BlockSpec((t, GLA_KEY_W), col(0)),
        pl.BlockSpec((t, GLA_KEY_W), col(1)),
        pl.BlockSpec((t, GLA_VAL_W), col(1)),
        pl.BlockSpec((t, ROPE_LANES), col(GLA_GATE_COL // ROPE_LANES)),
        pl.BlockSpec((ROPE_LANES, GLA_KEY_W), lambda b, j: (0, 0)),
        pl.BlockSpec((1, GLA_KEY_W), lambda b, j: (0, 0)),
    ]
    args = [proj, proj, proj, proj, wg, bg]
    scratch = [pltpu.VMEM((GLA_HEADS, GLA_DV, GLA_DK), F32)]
    if final:
        o_bwd, g_out, w_out, x = extra
        in_specs += [
            pl.BlockSpec((t, GLA_VAL_W), col(2)),
            pl.BlockSpec((t, GLA_VAL_W), col(0)),
            pl.BlockSpec((1, GLA_VAL_W), lambda b, j: (0, 0)),
            pl.BlockSpec((GLA_VAL_W, D_MODEL), lambda b, j: (0, 0)),
            pl.BlockSpec((t, D_MODEL), col(0)),
        ]
        args += [proj, o_bwd, g_out, w_out, x]
        scratch.append(pltpu.VMEM((t, GLA_VAL_W), BF16))
    return pl.pallas_call(
        functools.partial(_gla_kernel, bwd=bwd, final=final),
        grid=(batch, nb),
        in_specs=in_specs,
        out_specs=pl.BlockSpec((t, GLA_VAL_W), col(0)),
        out_shape=jax.ShapeDtypeStruct((n_tok, GLA_VAL_W), F32),
        scratch_shapes=scratch,
        compiler_params=_params(("parallel", "arbitrary")),
        name="gla_fwd_out" if final else "gla_bwd",
    )(*args)


def _ffn_kernel(*refs, tiles_per_seq, final):
    if final:
        (x_ref, xp_ref, xn_ref, g_ref, wup_ref, cpar_ref, wdown_ref, gfin_ref, o_ref,
         hn_ref, hnb_ref, acc_ref) = refs
    else:
        x_ref, xp_ref, xn_ref, g_ref, wup_ref, cpar_ref, wdown_ref, o_ref, hn_ref, hnb_ref, acc_ref = refs
    t = FFN_TILE
    ext = t + 2 * SUBLANES
    i = pl.program_id(0)
    g = g_ref[...]
    keep_prev = jnp.where(i % tiles_per_seq == 0, 0.0, 1.0)
    keep_next = jnp.where((i + 1) % tiles_per_seq == 0, 0.0, 1.0)
    hn_ref[0:SUBLANES, :] = _rms(xp_ref[...], g) * keep_prev
    hn_ref[SUBLANES:SUBLANES + t, :] = _rms(x_ref[...], g)
    hn_ref[SUBLANES + t:ext, :] = _rms(xn_ref[...], g) * keep_next
    hnb_ref[...] = hn_ref[...].astype(BF16)
    acc_ref[...] = x_ref[...]

    def up(c):
        return jnp.dot(hnb_ref[...], wup_ref[c], preferred_element_type=F32)

    ahead = [up(c) for c in range(FF_AHEAD)]
    for c in range(N_FF_CHUNKS):
        u = ahead.pop(0)
        if c + FF_AHEAD < N_FF_CHUNKS:
            ahead.append(up(c + FF_AHEAD))
        cp = cpar_ref[c]
        conv = (cp[0:1] * pltpu.roll(u, 1, 0) + cp[1:2] * u
                + cp[2:3] * pltpu.roll(u, ext - 1, 0) + cp[3:4])
        conv = conv[SUBLANES:SUBLANES + t]
        a = conv[:, :FF_CHUNK]
        gt = conv[:, FF_CHUNK:]
        act = (a * (gt / (1.0 + jnp.exp(-gt)))).astype(BF16)
        acc_ref[...] += jnp.dot(act, wdown_ref[c], preferred_element_type=F32)
    out = acc_ref[...]
    if final:
        out = _rms(out, gfin_ref[...])
    o_ref[...] = out


def conv_ffn(x, g, wup, cpar, wdown, seq, g_final=None):
    n_tok, d = x.shape
    t = FFN_TILE
    r8 = t // SUBLANES
    last8 = n_tok // SUBLANES - 1
    final = g_final is not None
    const3 = lambda i: (0, 0, 0)
    in_specs = [
        pl.BlockSpec((t, d), lambda i: (i, 0)),
        pl.BlockSpec((SUBLANES, d), lambda i: (jnp.maximum(i * r8 - 1, 0), 0)),
        pl.BlockSpec((SUBLANES, d), lambda i: (jnp.minimum((i + 1) * r8, last8), 0)),
        pl.BlockSpec((1, d), lambda i: (0, 0)),
        pl.BlockSpec(wup.shape, const3, pipeline_mode=pl.Buffered(1)),
        pl.BlockSpec(cpar.shape, const3, pipeline_mode=pl.Buffered(1)),
        pl.BlockSpec(wdown.shape, const3, pipeline_mode=pl.Buffered(1)),
    ]
    args = [x, x, x, g, wup, cpar, wdown]
    if final:
        in_specs.append(pl.BlockSpec((1, d), lambda i: (0, 0)))
        args.append(g_final)
    return pl.pallas_call(
        functools.partial(_ffn_kernel, tiles_per_seq=seq // t, final=final),
        grid=(n_tok // t,),
        in_specs=in_specs,
        out_specs=pl.BlockSpec((t, d), lambda i: (i, 0)),
        out_shape=jax.ShapeDtypeStruct((n_tok, d), F32),
        scratch_shapes=[pltpu.VMEM((t + 2 * SUBLANES, d), F32), pltpu.VMEM((t + 2 * SUBLANES, d), BF16),
                        pltpu.VMEM((t, d), F32)],
        compiler_params=_params(("parallel",)),
        name="conv_ffn_final" if final else "conv_ffn",
    )(*args)


def _mla_qkv_kernel(p_ref, gq_ref, gkv_ref, freq_ref, wn_ref, wr_ref, wrr_ref, wk_ref, wvt_ref,
                    q_ref, k_ref, vt_ref, *, tiles_per_seq):
    t = TOK_TILE
    i = pl.program_id(0)
    pos0 = (i % tiles_per_seq) * t
    pos = (pos0 + lax.broadcasted_iota(jnp.int32, (t, ROPE_LANES), 0)).astype(F32)
    ang = pos * freq_ref[...]
    cos = jnp.cos(ang)
    sin = jnp.sin(ang)

    cq = _rms(p_ref[:, 0:MLA_Q_RANK].astype(F32), gq_ref[...]).astype(BF16)
    ckv = _rms(p_ref[:, MLA_Q_RANK:MLA_Q_RANK + MLA_KV_RANK].astype(F32), gkv_ref[...]).astype(BF16)
    c0 = MLA_Q_RANK + MLA_KV_RANK
    kr = (p_ref[:, c0:c0 + ROPE_LANES].astype(F32) * cos
          + p_ref[:, c0 + ROPE_LANES:c0 + 2 * ROPE_LANES].astype(F32) * sin).astype(BF16)

    scale = (MLA_QK ** -0.5) * math.log2(math.e)
    nt_dims = (((1,), (1,)), ((), ()))
    cos_t = cos.T
    sin_t = sin.T
    qn = lax.dot_general(wn_ref[...], cq, nt_dims, preferred_element_type=F32) * scale
    qr = lax.dot_general(wr_ref[...], cq, nt_dims, preferred_element_type=F32)
    qrr = lax.dot_general(wrr_ref[...], cq, nt_dims, preferred_element_type=F32)
    kn = jnp.dot(ckv, wk_ref[...], preferred_element_type=F32)
    vt = lax.dot_general(wvt_ref[...], ckv, nt_dims, preferred_element_type=F32)
    for h in range(MLA_HEADS):
        hs = slice(h * MLA_NOPE, (h + 1) * MLA_NOPE)
        q_ref[h, 0, 0:MLA_NOPE, :] = qn[hs, :].astype(BF16)
        rr = slice(h * MLA_ROPE, (h + 1) * MLA_ROPE)
        q_ref[h, 0, MLA_NOPE:MLA_QK, :] = ((qr[rr, :] * cos_t[0:MLA_ROPE] + qrr[rr, :] * sin_t[0:MLA_ROPE])
                                           * scale).astype(BF16)
        q_ref[h, 0, MLA_QK:HEAD_PAD, :] = jnp.zeros((HEAD_PAD - MLA_QK, t), BF16)
        k_ref[h, :, 0:MLA_NOPE] = kn[:, hs].astype(BF16)
        k_ref[h, :, MLA_NOPE:HEAD_PAD] = kr
        vt_ref[h, 0, 0:MLA_V] = vt[hs, :].astype(BF16)
        vt_ref[h, 0, MLA_V:VT_ROWS] = jnp.ones((VT_ROWS - MLA_V, t), BF16)


def mla_qkv(proj, gq, gkv, freqs, wn, wr, wrr, wk, wvt, seq):
    n_tok = proj.shape[0]
    t = TOK_TILE
    nt = n_tok // t
    c2 = lambda i: (0, 0)
    return pl.pallas_call(
        functools.partial(_mla_qkv_kernel, tiles_per_seq=seq // t),
        grid=(nt,),
        in_specs=[
            pl.BlockSpec((t, MLA_IN_EXT), lambda i: (i, 0)),
            pl.BlockSpec(gq.shape, c2),
            pl.BlockSpec(gkv.shape, c2),
            pl.BlockSpec(freqs.shape, c2),
            pl.BlockSpec(wn.shape, c2),
            pl.BlockSpec(wr.shape, c2),
            pl.BlockSpec(wrr.shape, c2),
            pl.BlockSpec(wk.shape, c2),
            pl.BlockSpec(wvt.shape, c2),
        ],
        out_specs=[
            pl.BlockSpec((MLA_HEADS, 1, HEAD_PAD, t), lambda i: (0, i, 0, 0)),
            pl.BlockSpec((MLA_HEADS, t, HEAD_PAD), lambda i: (0, i, 0)),
            pl.BlockSpec((MLA_HEADS, 1, VT_ROWS, t), lambda i: (0, i, 0, 0)),
        ],
        out_shape=[
            jax.ShapeDtypeStruct((MLA_HEADS, nt, HEAD_PAD, t), BF16),
            jax.ShapeDtypeStruct((MLA_HEADS, n_tok, HEAD_PAD), BF16),
            jax.ShapeDtypeStruct((MLA_HEADS, nt, VT_ROWS, t), BF16),
        ],
        compiler_params=_params(("parallel",)),
        name="mla_qkv",
    )(proj, gq, gkv, freqs, wn, wr, wrr, wk, wvt)


def _attn_kernel(q_ref, k_ref, vt_ref, o_ref, sa_ref, sb_ref, *, n_span, subs, n_q):
    span = subs * ATT_SUB

    def qk_block(qi, j, b, dst_ref, cm):
        row0 = pl.multiple_of(j * span + b * ATT_SUB, ATT_SUB)
        s = jnp.dot(k_ref[pl.ds(row0, ATT_SUB), :], q_ref[qi], preferred_element_type=F32)
        dst_ref[b * ATT_SUB:(b + 1) * ATT_SUB, :] = s
        c = jnp.max(s, axis=0, keepdims=True)
        return c if cm is None else jnp.maximum(cm, c)

    def pv_block(j, b, src_ref, m_new, pv):
        p = jnp.exp2(src_ref[b * ATT_SUB:(b + 1) * ATT_SUB, :] - m_new).astype(BF16)
        blk, off = divmod(b * ATT_SUB, ATT_TK)
        vt = vt_ref[j * (span // ATT_TK) + blk, :, off:off + ATT_SUB]
        d = jnp.dot(vt, p, preferred_element_type=F32)
        return d if pv is None else pv + d

    def step(j, cur_ref, nxt_ref, q_next, j_next, cm, m, acc):
        m_new = jnp.maximum(m, cm)
        alpha = jnp.exp2(m - m_new)
        pv = None
        cm_next = None
        for b in range(subs):
            cm_next = qk_block(q_next, j_next, b, nxt_ref, cm_next)
            pv = pv_block(j, b, cur_ref, m_new, pv)
        return cm_next, m_new, alpha * acc + pv

    def finish(qi, acc):
        row0 = pl.multiple_of(qi * ATT_TQ, ATT_TQ)
        o_ref[pl.ds(row0, ATT_TQ), :] = (acc[:MLA_V] / acc[MLA_V:MLA_V + 1]).T.astype(o_ref.dtype)

    m0 = jnp.full((1, ATT_TQ), -jnp.inf, F32)
    acc0 = jnp.zeros((VT_ROWS, ATT_TQ), F32)

    def q_tile(qi, cm):
        def pair(jj, carry):
            cm, m, acc = carry
            cm, m, acc = step(2 * jj, sa_ref, sb_ref, qi, 2 * jj + 1, cm, m, acc)
            return step(2 * jj + 1, sb_ref, sa_ref, qi, 2 * jj + 2, cm, m, acc)

        cm, m, acc = lax.fori_loop(0, n_span // 2 - 1, pair, (cm, m0, acc0))
        cm, m, acc = step(n_span - 2, sa_ref, sb_ref, qi, n_span - 1, cm, m, acc)
        cm, _, acc = step(n_span - 1, sb_ref, sa_ref, jnp.minimum(qi + 1, n_q - 1), 0, cm, m, acc)
        finish(qi, acc)
        return cm

    cm = None
    for b in range(subs):
        cm = qk_block(0, 0, b, sa_ref, cm)
    lax.fori_loop(0, n_q, q_tile, cm)


def attention(q, k, vt, batch, seq):
    n_tok = k.shape[1]
    n_q = min(ATT_NQ, seq // ATT_TQ)
    nq = seq // (n_q * ATT_TQ)
    n_kv = seq // ATT_TK
    subs = ATT_SUBS if seq >= 4 * ATT_SUBS * ATT_SUB else 2
    span = subs * ATT_SUB
    n_span = seq // span
    assert n_span % 2 == 0 and span % ATT_TK == 0
    return pl.pallas_call(
        functools.partial(_attn_kernel, n_span=n_span, subs=subs, n_q=n_q),
        grid=(batch, MLA_HEADS, nq),
        in_specs=[
            pl.BlockSpec((None, n_q, HEAD_PAD, ATT_TQ), lambda b, h, i: (h, b * nq + i, 0, 0)),
            pl.BlockSpec((None, seq, HEAD_PAD), lambda b, h, i: (h, b, 0)),
            pl.BlockSpec((None, n_kv, VT_ROWS, ATT_TK), lambda b, h, i: (h, b, 0, 0)),
        ],
        out_specs=pl.BlockSpec((n_q * ATT_TQ, MLA_V), lambda b, h, i: (b * nq + i, h)),
        out_shape=jax.ShapeDtypeStruct((n_tok, MLA_HEADS * MLA_V), BF16),
        scratch_shapes=[pltpu.VMEM((span, ATT_TQ), F32), pltpu.VMEM((span, ATT_TQ), F32)],
        compiler_params=_params(("parallel", "parallel", "arbitrary")),
        name="mla_attention",
    )(q, k, vt)


def _prep_gla(w_in, w_gate_up, b_gate, g_out, w_out):
    w_in_p = jnp.pad(w_in, ((0, 0), (0, GLA_IN_PAD - GLA_IN_W))).astype(BF16)
    wg = jnp.zeros((2, ROPE_LANES, GLA_KEY_W), F32)
    wg = wg.at[0, 0:GLA_GATE_RANK].set(w_gate_up[0])
    wg = wg.at[1, GLA_GATE_RANK:2 * GLA_GATE_RANK].set(w_gate_up[1])
    return dict(w_in=w_in_p, wg=wg.astype(BF16), bg=b_gate.reshape(2, 1, GLA_KEY_W),
                g_out=g_out.reshape(1, GLA_VAL_W), w_out=w_out.astype(BF16))


def _prep_mla(w_in, g_q, w_uq, g_kv, w_ukv, w_out):
    c0 = MLA_Q_RANK + MLA_KV_RANK
    half = MLA_ROPE // 2
    zpad = jnp.zeros((D_MODEL, ROPE_LANES - MLA_ROPE), F32)
    w_in_ext = jnp.concatenate(
        [w_in, zpad, -w_in[:, c0 + half:c0 + MLA_ROPE], w_in[:, c0:c0 + half], zpad], axis=1).astype(BF16)
    uq = w_uq.reshape(MLA_Q_RANK, MLA_HEADS, MLA_QK)
    rope = uq[:, :, MLA_NOPE:]
    wn = uq[:, :, :MLA_NOPE].reshape(MLA_Q_RANK, -1).T.astype(BF16)
    wr = rope.reshape(MLA_Q_RANK, -1).T.astype(BF16)
    wrr = jnp.concatenate([-rope[:, :, half:], rope[:, :, :half]], axis=2).reshape(MLA_Q_RANK, -1).T.astype(BF16)
    ukv = w_ukv.reshape(MLA_KV_RANK, MLA_HEADS, MLA_NOPE + MLA_V)
    wk = ukv[:, :, :MLA_NOPE].reshape(MLA_KV_RANK, -1).astype(BF16)
    wvt = ukv[:, :, MLA_NOPE:].reshape(MLA_KV_RANK, -1).T.astype(BF16)
    freqs = ROPE_THETA ** (-jnp.arange(half, dtype=F32) / half)
    freqs = jnp.tile(freqs, ROPE_LANES // half).reshape(1, ROPE_LANES)
    return dict(w_in=w_in_ext, g_q=g_q.reshape(1, -1), g_kv=g_kv.reshape(1, -1), freqs=freqs,
                wn=wn, wr=wr, wrr=wrr, wk=wk, wvt=wvt, w_out=w_out.astype(BF16))


def _prep_ffn(w_up, conv_w, conv_b, w_down):
    def split(m):
        lead = m.shape[:-1]
        a = m[..., :D_FF].reshape(*lead, N_FF_CHUNKS, FF_CHUNK)
        g = m[..., D_FF:].reshape(*lead, N_FF_CHUNKS, FF_CHUNK)
        return jnp.concatenate([a, g], axis=-1)

    wup = jnp.transpose(split(w_up), (1, 0, 2)).astype(BF16)
    taps = jnp.transpose(split(conv_w.reshape(3, 2 * D_FF)), (1, 0, 2))
    bias = split(conv_b.reshape(1, 2 * D_FF)).transpose(1, 0, 2)
    cpar = jnp.concatenate([taps, bias, jnp.zeros((N_FF_CHUNKS, 4, 2 * FF_CHUNK), F32)], axis=1)
    wdown = w_down.reshape(N_FF_CHUNKS, FF_CHUNK, D_MODEL).astype(BF16)
    return dict(wup=wup, cpar=cpar, wdown=wdown)


def _trunk(x3, norm_mix, norm_ffn, norm_final, gla, mla, ffn):
    batch, seq, d = x3.shape
    x = x3.reshape(batch * seq, d)

    proj = norm_matmul(x, norm_mix[0:1], gla["w_in"], "gla_in_proj")
    o_bwd = gla_pass(proj, gla["wg"][1], gla["bg"][1], batch, seq, bwd=True)
    x = gla_pass(proj, gla["wg"][0], gla["bg"][0], batch, seq, bwd=False,
                 extra=(o_bwd, gla["g_out"], gla["w_out"], x))
    x = conv_ffn(x, norm_ffn[0:1], ffn[0]["wup"], ffn[0]["cpar"], ffn[0]["wdown"], seq)

    proj = norm_matmul(x, norm_mix[1:2], mla["w_in"], "mla_in_proj")
    q, k, vt = mla_qkv(proj, mla["g_q"], mla["g_kv"], mla["freqs"], mla["wn"], mla["wr"], mla["wrr"],
                       mla["wk"], mla["wvt"], seq)
    att = attention(q, k, vt, batch, seq)
    x = matmul_residual(x, att, mla["w_out"], "mla_out_proj")
    x = conv_ffn(x, norm_ffn[1:2], ffn[1]["wup"], ffn[1]["cpar"], ffn[1]["wdown"], seq,
                 g_final=norm_final.reshape(1, d))
    return x.reshape(batch, seq, d)


def kernel(x_prompt, x_sample, norm_mix, norm_ffn, norm_final, gla_w_in, gla_w_gate_up, gla_b_gate, gla_g_out, gla_w_out, mla_w_in, mla_g_q, mla_w_uq, mla_g_kv, mla_w_ukv, mla_w_out, ffn_w_up, ffn_conv_w, ffn_conv_b, ffn_w_down):
    gla = _prep_gla(gla_w_in[0], gla_w_gate_up[0], gla_b_gate[0], gla_g_out[0], gla_w_out[0])
    mla = _prep_mla(mla_w_in[0], mla_g_q[0], mla_w_uq[0], mla_g_kv[0], mla_w_ukv[0], mla_w_out[0])
    ffn = [_prep_ffn(ffn_w_up[i], ffn_conv_w[i], ffn_conv_b[i], ffn_w_down[i]) for i in range(2)]
    y_prompt = _trunk(x_prompt, norm_mix, norm_ffn, norm_final, gla, mla, ffn)
    y_sample = _trunk(x_sample, norm_mix, norm_ffn, norm_final, gla, mla, ffn)
    return (y_prompt, y_sample)
```

```python
import functools
import math

import jax
import jax.numpy as jnp
from jax import lax
from jax.experimental import pallas as pl
from jax.experimental.pallas import tpu as pltpu

F32 = jnp.float32
BF16 = jnp.bfloat16

D_MODEL = 1024
EPS = 1e-6

GLA_HEADS = 4
GLA_DK = 128
GLA_DV = 256
GLA_KEY_W = GLA_HEADS * GLA_DK
GLA_VAL_W = GLA_HEADS * GLA_DV
GLA_GATE_RANK = 16
GLA_GATE_TAU = 16.0
GLA_CHUNK = 64
GLA_IN_W = 2 * GLA_KEY_W + 2 * GLA_VAL_W + 2 * GLA_GATE_RANK
GLA_IN_PAD = 3200
GLA_GATE_COL = 2 * GLA_KEY_W + 2 * GLA_VAL_W

MLA_HEADS = 16
MLA_Q_RANK = 384
MLA_KV_RANK = 256
MLA_NOPE = 128
MLA_ROPE = 64
MLA_V = 128
MLA_QK = MLA_NOPE + MLA_ROPE
MLA_IN_W = MLA_Q_RANK + MLA_KV_RANK + MLA_ROPE
MLA_IN_EXT = 896
ROPE_THETA = 10000.0
ROPE_LANES = 128
HEAD_PAD = 256
VT_ROWS = MLA_V + 16

D_FF = 2816
FF_CHUNK = 256
N_FF_CHUNKS = D_FF // FF_CHUNK
FF_AHEAD = 3

SUBLANES = 8
VMEM_LIMIT = 56 * 1024 * 1024

TOK_TILE = 512
FFN_TILE = 256
GLA_TILE = 512
GLA_BLOCK = 256
ATT_TQ = 512
ATT_TK = 512
ATT_SUB = 256
ATT_SUBS = 8
ATT_NQ = 8
ATT_SHORT_HEADS = 2


def _rms(x, g):
    return x * lax.rsqrt(jnp.mean(x * x, axis=-1, keepdims=True) + EPS) * g


def _params(sem):
    return pltpu.CompilerParams(dimension_semantics=sem, vmem_limit_bytes=VMEM_LIMIT)


def _norm_matmul_kernel(x_ref, g_ref, w_ref, o_ref):
    hn = _rms(x_ref[...], g_ref[...]).astype(BF16)
    o_ref[...] = jnp.dot(hn, w_ref[...], preferred_element_type=F32).astype(o_ref.dtype)


def norm_matmul(x, g, w, name):
    n_tok, d = x.shape
    n_out = w.shape[1]
    return pl.pallas_call(
        _norm_matmul_kernel,
        grid=(n_tok // TOK_TILE,),
        in_specs=[
            pl.BlockSpec((TOK_TILE, d), lambda i: (i, 0)),
            pl.BlockSpec((1, d), lambda i: (0, 0)),
            pl.BlockSpec((d, n_out), lambda i: (0, 0)),
        ],
        out_specs=pl.BlockSpec((TOK_TILE, n_out), lambda i: (i, 0)),
        out_shape=jax.ShapeDtypeStruct((n_tok, n_out), BF16),
        compiler_params=_params(("parallel",)),
        name=name,
    )(x, g, w)


def _matmul_residual_kernel(x_ref, a_ref, w_ref, o_ref):
    o_ref[...] = x_ref[...] + jnp.dot(a_ref[...], w_ref[...], preferred_element_type=F32)


def matmul_residual(x, a, w, name):
    n_tok, d = x.shape
    k = a.shape[1]
    return pl.pallas_call(
        _matmul_residual_kernel,
        grid=(n_tok // TOK_TILE,),
        in_specs=[
            pl.BlockSpec((TOK_TILE, d), lambda i: (i, 0)),
            pl.BlockSpec((TOK_TILE, k), lambda i: (i, 0)),
            pl.BlockSpec((k, d), lambda i: (0, 0)),
        ],
        out_specs=pl.BlockSpec((TOK_TILE, d), lambda i: (i, 0)),
        out_shape=jax.ShapeDtypeStruct((n_tok, d), F32),
        compiler_params=_params(("parallel",)),
        name=name,
    )(x, a, w)


def _gla_kernel(*refs, bwd, final):
    if final:
        (q_ref, k_ref, v_ref, g_ref, wg_ref, bg_ref, r_ref, ob_ref, gout_ref, wout_ref, x_ref,
         o_ref, state_ref, gated_ref) = refs
    else:
        q_ref, k_ref, v_ref, g_ref, wg_ref, bg_ref, o_ref, state_ref = refs
    t = GLA_BLOCK
    n_chunks = t // GLA_CHUNK
    blocks = range(GLA_TILE // t)

    @pl.when(pl.program_id(1) == 0)
    def _():
        state_ref[...] = jnp.zeros_like(state_ref)

    rows = lax.broadcasted_iota(jnp.int32, (t, t), 0)
    cols = lax.broadcasted_iota(jnp.int32, (t, t), 1)
    shift = GLA_CHUNK.bit_length() - 1
    same = (rows >> shift) == (cols >> shift)
    if bwd:
        cum_m = same & (cols >= rows)
        att_m = same & (cols > rows)
    else:
        cum_m = same & (cols <= rows)
        att_m = same & (cols <= rows)
    cum_mat = jnp.where(cum_m, 1.0, 0.0).astype(BF16)
    tot_mat = jnp.where(same, 1.0, 0.0).astype(BF16)

    blk_rows = [slice(i * t, (i + 1) * t) for i in blocks]
    z = [jnp.dot(g_ref[rs, :], wg_ref[...], preferred_element_type=F32) + bg_ref[...] for rs in blk_rows]
    la = [-(jnp.maximum(-zi, 0.0) + jnp.log(1.0 + jnp.exp(-jnp.abs(zi)))) * (1.0 / GLA_GATE_TAU) for zi in z]
    la_hi = [x.astype(BF16) for x in la]
    la_lo = [(x - h.astype(F32)).astype(BF16) for x, h in zip(la, la_hi)]
    cum = [jnp.dot(cum_mat, h, preferred_element_type=F32) + jnp.dot(cum_mat, l, preferred_element_type=F32)
           for h, l in zip(la_hi, la_lo)]
    tot = [jnp.dot(tot_mat, h, preferred_element_type=F32) + jnp.dot(tot_mat, l, preferred_element_type=F32)
           for h, l in zip(la_hi, la_lo)]

    q = [q_ref[rs, :].astype(F32) * (GLA_DK ** -0.5) for rs in blk_rows]
    k = [k_ref[rs, :].astype(F32) for rs in blk_rows]
    q_dec = [(q[i] * jnp.exp(cum[i])).astype(BF16) for i in blocks]
    k_inv = [(k[i] * jnp.exp(-cum[i])).astype(BF16) for i in blocks]
    k_end = [(k[i] * jnp.exp(tot[i] - cum[i])).astype(BF16) for i in blocks]
    decay = [jnp.exp(tot[i]) for i in blocks]

    heads = range(GLA_HEADS)
    ks = [slice(h * GLA_DK, (h + 1) * GLA_DK) for h in heads]
    vs = [slice(h * GLA_DV, (h + 1) * GLA_DV) for h in heads]
    nt_dims = (((1,), (1,)), ((), ()))
    o_intra = [[None] * GLA_HEADS for _ in blocks]
    for h in heads:
        for i in blocks:
            sc = lax.dot_general(q_dec[i][:, ks[h]], k_inv[i][:, ks[h]], nt_dims, preferred_element_type=F32)
            p = jnp.where(att_m, sc, 0.0).astype(BF16)
            o_intra[i][h] = jnp.dot(p, v_ref[blk_rows[i], vs[h]], preferred_element_type=F32)

    st = [state_ref[h] for h in heads]
    for i in (reversed(blocks) if bwd else blocks):
        for n in (range(n_chunks - 1, -1, -1) if bwd else range(n_chunks)):
            cs = slice(n * GLA_CHUNK, (n + 1) * GLA_CHUNK)
            rs = slice(i * t + n * GLA_CHUNK, i * t + (n + 1) * GLA_CHUNK)
            for h in heads:
                o_inter = lax.dot_general(q_dec[i][cs, ks[h]], st[h].astype(BF16), nt_dims,
                                          preferred_element_type=F32)
                kv_t = lax.dot_general(v_ref[rs, vs[h]], k_end[i][cs, ks[h]], (((0,), (0,)), ((), ())),
                                       preferred_element_type=F32)
                st[h] = decay[i][n * GLA_CHUNK:n * GLA_CHUNK + 1, ks[h]] * st[h] + kv_t
                o_n = o_intra[i][h][cs] + o_inter
                if final:
                    o_n = o_n + ob_ref[rs, vs[h]]
                    o_n = o_n * lax.rsqrt(jnp.mean(o_n * o_n, axis=-1, keepdims=True) + EPS) * gout_ref[:, vs[h]]
                    r = r_ref[rs, vs[h]].astype(F32)
                    o_n = o_n * (r / (1.0 + jnp.exp(-r)))
                    gated_ref[rs, vs[h]] = o_n.astype(BF16)
                else:
                    o_ref[rs, vs[h]] = o_n
    for h in heads:
        state_ref[h] = st[h]

    if final:
        o_ref[...] = x_ref[...] + jnp.dot(gated_ref[...], wout_ref[...], preferred_element_type=F32)


def gla_pass(proj, wg, bg, batch, seq, *, bwd, extra=None):
    n_tok = proj.shape[0]
    t = GLA_TILE
    nb = seq // t
    final = extra is not None

    def row(b, j):
        return b * nb + ((nb - 1 - j) if bwd else j)

    def col(c):
        return lambda b, j: (row(b, j), c)

    in_specs = [
        pl.BlockSpec((t, GLA_KEY_W), col(0)),
        pl.BlockSpec((t, GLA_KEY_W), col(1)),
        pl.BlockSpec((t, GLA_VAL_W), col(1)),
        pl.BlockSpec((t, ROPE_LANES), col(GLA_GATE_COL // ROPE_LANES)),
        pl.BlockSpec((ROPE_LANES, GLA_KEY_W), lambda b, j: (0, 0)),
        pl.BlockSpec((1, GLA_KEY_W), lambda b, j: (0, 0)),
    ]
    args = [proj, proj, proj, proj, wg, bg]
    scratch = [pltpu.VMEM((GLA_HEADS, GLA_DV, GLA_DK), F32)]
    if final:
        o_bwd, g_out, w_out, x = extra
        in_specs += [
            pl.BlockSpec((t, GLA_VAL_W), col(2)),
            pl.BlockSpec((t, GLA_VAL_W), col(0)),
            pl.BlockSpec((1, GLA_VAL_W), lambda b, j: (0, 0)),
            pl.BlockSpec((GLA_VAL_W, D_MODEL), lambda b, j: (0, 0)),
            pl.BlockSpec((t, D_MODEL), col(0)),
        ]
        args += [proj, o_bwd, g_out, w_out, x]
        scratch.append(pltpu.VMEM((t, GLA_VAL_W), BF16))
    return pl.pallas_call(
        functools.partial(_gla_kernel, bwd=bwd, final=final),
        grid=(batch, nb),
        in_specs=in_specs,
        out_specs=pl.BlockSpec((t, GLA_VAL_W), col(0)),
        out_shape=jax.ShapeDtypeStruct((n_tok, GLA_VAL_W), F32),
        scratch_shapes=scratch,
        compiler_params=_params(("parallel", "arbitrary")),
        name="gla_fwd_out" if final else "gla_bwd",
    )(*args)


def _ffn_kernel(*refs, tiles_per_seq, final):
    if final:
        (x_ref, xp_ref, xn_ref, g_ref, wup_ref, cpar_ref, wdown_ref, gfin_ref, o_ref,
         hn_ref, hnb_ref, acc_ref) = refs
    else:
        x_ref, xp_ref, xn_ref, g_ref, wup_ref, cpar_ref, wdown_ref, o_ref, hn_ref, hnb_ref, acc_ref = refs
    t = FFN_TILE
    ext = t + 2 * SUBLANES
    i = pl.program_id(0)
    g = g_ref[...]
    keep_prev = jnp.where(i % tiles_per_seq == 0, 0.0, 1.0)
    keep_next = jnp.where((i + 1) % tiles_per_seq == 0, 0.0, 1.0)
    hn_ref[0:SUBLANES, :] = _rms(xp_ref[...], g) * keep_prev
    hn_ref[SUBLANES:SUBLANES + t, :] = _rms(x_ref[...], g)
    hn_ref[SUBLANES + t:ext, :] = _rms(xn_ref[...], g) * keep_next
    hnb_ref[...] = hn_ref[...].astype(BF16)
    acc_ref[...] = x_ref[...]

    def up(c):
        return jnp.dot(hnb_ref[...], wup_ref[c], preferred_element_type=F32)

    ahead = [up(c) for c in range(FF_AHEAD)]
    for c in range(N_FF_CHUNKS):
        u = ahead.pop(0)
        if c + FF_AHEAD < N_FF_CHUNKS:
            ahead.append(up(c + FF_AHEAD))
        cp = cpar_ref[c]
        conv = (cp[0:1] * pltpu.roll(u, 1, 0) + cp[1:2] * u
                + cp[2:3] * pltpu.roll(u, ext - 1, 0) + cp[3:4])
        conv = conv[SUBLANES:SUBLANES + t]
        a = conv[:, :FF_CHUNK]
        gt = conv[:, FF_CHUNK:]
        act = (a * (gt / (1.0 + jnp.exp(-gt)))).astype(BF16)
        acc_ref[...] += jnp.dot(act, wdown_ref[c], preferred_element_type=F32)
    out = acc_ref[...]
    if final:
        out = _rms(out, gfin_ref[...])
    o_ref[...] = out


def conv_ffn(x, g, wup, cpar, wdown, seq, g_final=None):
    n_tok, d = x.shape
    t = FFN_TILE
    r8 = t // SUBLANES
    last8 = n_tok // SUBLANES - 1
    final = g_final is not None
    const3 = lambda i: (0, 0, 0)
    in_specs = [
        pl.BlockSpec((t, d), lambda i: (i, 0)),
        pl.BlockSpec((SUBLANES, d), lambda i: (jnp.maximum(i * r8 - 1, 0), 0)),
        pl.BlockSpec((SUBLANES, d), lambda i: (jnp.minimum((i + 1) * r8, last8), 0)),
        pl.BlockSpec((1, d), lambda i: (0, 0)),
        pl.BlockSpec(wup.shape, const3, pipeline_mode=pl.Buffered(1)),
        pl.BlockSpec(cpar.shape, const3, pipeline_mode=pl.Buffered(1)),
        pl.BlockSpec(wdown.shape, const3, pipeline_mode=pl.Buffered(1)),
    ]
    args = [x, x, x, g, wup, cpar, wdown]
    if final:
        in_specs.append(pl.BlockSpec((1, d), lambda i: (0, 0)))
        args.append(g_final)
    return pl.pallas_call(
        functools.partial(_ffn_kernel, tiles_per_seq=seq // t, final=final),
        grid=(n_tok // t,),
        in_specs=in_specs,
        out_specs=pl.BlockSpec((t, d), lambda i: (i, 0)),
        out_shape=jax.ShapeDtypeStruct((n_tok, d), F32),
        scratch_shapes=[pltpu.VMEM((t + 2 * SUBLANES, d), F32), pltpu.VMEM((t + 2 * SUBLANES, d), BF16),
                        pltpu.VMEM((t, d), F32)],
        compiler_params=_params(("parallel",)),
        name="conv_ffn_final" if final else "conv_ffn",
    )(*args)


def _mla_qkv_kernel(p_ref, gq_ref, gkv_ref, freq_ref, wn_ref, wr_ref, wrr_ref, wk_ref, wvt_ref,
                    q_ref, k_ref, vt_ref, *, tiles_per_seq):
    t = TOK_TILE
    i = pl.program_id(0)
    pos0 = (i % tiles_per_seq) * t
    pos = (pos0 + lax.broadcasted_iota(jnp.int32, (t, ROPE_LANES), 0)).astype(F32)
    ang = pos * freq_ref[...]
    cos = jnp.cos(ang)
    sin = jnp.sin(ang)

    cq = _rms(p_ref[:, 0:MLA_Q_RANK].astype(F32), gq_ref[...]).astype(BF16)
    ckv = _rms(p_ref[:, MLA_Q_RANK:MLA_Q_RANK + MLA_KV_RANK].astype(F32), gkv_ref[...]).astype(BF16)
    c0 = MLA_Q_RANK + MLA_KV_RANK
    kr = (p_ref[:, c0:c0 + ROPE_LANES].astype(F32) * cos
          + p_ref[:, c0 + ROPE_LANES:c0 + 2 * ROPE_LANES].astype(F32) * sin).astype(BF16)

    scale = (MLA_QK ** -0.5) * math.log2(math.e)
    nt_dims = (((1,), (1,)), ((), ()))
    cos_t = cos.T
    sin_t = sin.T
    qn = lax.dot_general(wn_ref[...], cq, nt_dims, preferred_element_type=F32) * scale
    qr = lax.dot_general(wr_ref[...], cq, nt_dims, preferred_element_type=F32)
    qrr = lax.dot_general(wrr_ref[...], cq, nt_dims, preferred_element_type=F32)
    kn = jnp.dot(ckv, wk_ref[...], preferred_element_type=F32)
    vt = lax.dot_general(wvt_ref[...], ckv, nt_dims, preferred_element_type=F32)
    for h in range(MLA_HEADS):
        hs = slice(h * MLA_NOPE, (h + 1) * MLA_NOPE)
        q_ref[h, 0, 0:MLA_NOPE, :] = qn[hs, :].astype(BF16)
        rr = slice(h * MLA_ROPE, (h + 1) * MLA_ROPE)
        q_ref[h, 0, MLA_NOPE:MLA_QK, :] = ((qr[rr, :] * cos_t[0:MLA_ROPE] + qrr[rr, :] * sin_t[0:MLA_ROPE])
                                           * scale).astype(BF16)
        q_ref[h, 0, MLA_QK:HEAD_PAD, :] = jnp.zeros((HEAD_PAD - MLA_QK, t), BF16)
        k_ref[h, :, 0:MLA_NOPE] = kn[:, hs].astype(BF16)
        k_ref[h, :, MLA_NOPE:HEAD_PAD] = kr
        vt_ref[h, 0, 0:MLA_V] = vt[hs, :].astype(BF16)
        vt_ref[h, 0, MLA_V:VT_ROWS] = jnp.ones((VT_ROWS - MLA_V, t), BF16)


def mla_qkv(proj, gq, gkv, freqs, wn, wr, wrr, wk, wvt, seq):
    n_tok = proj.shape[0]
    t = TOK_TILE
    nt = n_tok // t
    c2 = lambda i: (0, 0)
    return pl.pallas_call(
        functools.partial(_mla_qkv_kernel, tiles_per_seq=seq // t),
        grid=(nt,),
        in_specs=[
            pl.BlockSpec((t, MLA_IN_EXT), lambda i: (i, 0)),
            pl.BlockSpec(gq.shape, c2),
            pl.BlockSpec(gkv.shape, c2),
            pl.BlockSpec(freqs.shape, c2),
            pl.BlockSpec(wn.shape, c2),
            pl.BlockSpec(wr.shape, c2),
            pl.BlockSpec(wrr.shape, c2),
            pl.BlockSpec(wk.shape, c2),
            pl.BlockSpec(wvt.shape, c2),
        ],
        out_specs=[
            pl.BlockSpec((MLA_HEADS, 1, HEAD_PAD, t), lambda i: (0, i, 0, 0)),
            pl.BlockSpec((MLA_HEADS, t, HEAD_PAD), lambda i: (0, i, 0)),
            pl.BlockSpec((MLA_HEADS, 1, VT_ROWS, t), lambda i: (0, i, 0, 0)),
        ],
        out_shape=[
            jax.ShapeDtypeStruct((MLA_HEADS, nt, HEAD_PAD, t), BF16),
            jax.ShapeDtypeStruct((MLA_HEADS, n_tok, HEAD_PAD), BF16),
            jax.ShapeDtypeStruct((MLA_HEADS, nt, VT_ROWS, t), BF16),
        ],
        compiler_params=_params(("parallel",)),
        name="mla_qkv",
    )(proj, gq, gkv, freqs, wn, wr, wrr, wk, wvt)


def _attn_kernel(q_ref, k_ref, vt_ref, o_ref, sa_ref, sb_ref, *, n_span, subs, n_q):
    span = subs * ATT_SUB

    def qk_block(qi, j, b, dst_ref, cm):
        row0 = pl.multiple_of(j * span + b * ATT_SUB, ATT_SUB)
        s = jnp.dot(k_ref[pl.ds(row0, ATT_SUB), :], q_ref[qi], preferred_element_type=F32)
        dst_ref[b * ATT_SUB:(b + 1) * ATT_SUB, :] = s
        c = jnp.max(s, axis=0, keepdims=True)
        return c if cm is None else jnp.maximum(cm, c)

    def pv_block(j, b, src_ref, m_new, pv):
        p = jnp.exp2(src_ref[b * ATT_SUB:(b + 1) * ATT_SUB, :] - m_new).astype(BF16)
        blk, off = divmod(b * ATT_SUB, ATT_TK)
        vt = vt_ref[j * (span // ATT_TK) + blk, :, off:off + ATT_SUB]
        d = jnp.dot(vt, p, preferred_element_type=F32)
        return d if pv is None else pv + d

    def step(j, cur_ref, nxt_ref, q_next, j_next, cm, m, acc):
        m_new = jnp.maximum(m, cm)
        alpha = jnp.exp2(m - m_new)
        pv = None
        cm_next = None
        for b in range(subs):
            cm_next = qk_block(q_next, j_next, b, nxt_ref, cm_next)
            pv = pv_block(j, b, cur_ref, m_new, pv)
        return cm_next, m_new, alpha * acc + pv

    def finish(qi, acc):
        row0 = pl.multiple_of(qi * ATT_TQ, ATT_TQ)
        o_ref[pl.ds(row0, ATT_TQ), :] = (acc[:MLA_V] / acc[MLA_V:MLA_V + 1]).T.astype(o_ref.dtype)

    m0 = jnp.full((1, ATT_TQ), -jnp.inf, F32)
    acc0 = jnp.zeros((VT_ROWS, ATT_TQ), F32)

    def q_tile(qi, cm):
        def pair(jj, carry):
            cm, m, acc = carry
            cm, m, acc = step(2 * jj, sa_ref, sb_ref, qi, 2 * jj + 1, cm, m, acc)
            return step(2 * jj + 1, sb_ref, sa_ref, qi, 2 * jj + 2, cm, m, acc)

        cm, m, acc = lax.fori_loop(0, n_span // 2 - 1, pair, (cm, m0, acc0))
        cm, m, acc = step(n_span - 2, sa_ref, sb_ref, qi, n_span - 1, cm, m, acc)
        cm, _, acc = step(n_span - 1, sb_ref, sa_ref, jnp.minimum(qi + 1, n_q - 1), 0, cm, m, acc)
        finish(qi, acc)
        return cm

    cm = None
    for b in range(subs):
        cm = qk_block(0, 0, b, sa_ref, cm)
    lax.fori_loop(0, n_q, q_tile, cm)


def _attn_short_kernel(q_ref, k_ref, vt_ref, o_ref, sa_ref, sb_ref, *, subs, n_q):
    def qk_block(hh, qi, b, dst_ref, cm):
        rows = slice(b * ATT_SUB, (b + 1) * ATT_SUB)
        s = jnp.dot(k_ref[hh, rows, :], q_ref[hh, qi], preferred_element_type=F32)
        dst_ref[rows, :] = s
        c = jnp.max(s, axis=0, keepdims=True)
        return c if cm is None else jnp.maximum(cm, c)

    def pv_block(hh, b, src_ref, cm, pv):
        p = jnp.exp2(src_ref[b * ATT_SUB:(b + 1) * ATT_SUB, :] - cm).astype(BF16)
        blk, off = divmod(b * ATT_SUB, ATT_TK)
        d = jnp.dot(vt_ref[hh, blk, :, off:off + ATT_SUB], p, preferred_element_type=F32)
        return d if pv is None else pv + d

    def step(hh, qi, cur_ref, nxt_ref, hh_next, q_next, cm):
        pv = None
        cm_next = None
        for b in range(subs):
            cm_next = qk_block(hh_next, q_next, b, nxt_ref, cm_next)
            pv = pv_block(hh, b, cur_ref, cm, pv)
        row0 = pl.multiple_of(qi * ATT_TQ, ATT_TQ)
        o_ref[pl.ds(row0, ATT_TQ), hh * MLA_V:(hh + 1) * MLA_V] = (
            pv[:MLA_V] / pv[MLA_V:MLA_V + 1]).T.astype(o_ref.dtype)
        return cm_next

    cm = None
    for b in range(subs):
        cm = qk_block(0, 0, b, sa_ref, cm)
    for hh in range(ATT_SHORT_HEADS):
        def pair(i, cm, hh=hh):
            q0 = 2 * i
            cm = step(hh, q0, sa_ref, sb_ref, hh, q0 + 1, cm)
            wrap = q0 + 2 >= n_q
            hh_next = jnp.where(wrap, min(hh + 1, ATT_SHORT_HEADS - 1), hh)
            q_next = jnp.where(wrap, 0, q0 + 2)
            return step(hh, q0 + 1, sb_ref, sa_ref, hh_next, q_next, cm)

        cm = lax.fori_loop(0, n_q // 2, pair, cm)


def attention_short(q, k, vt, batch, seq):
    n_tok = k.shape[1]
    n_q = seq // ATT_TQ
    n_kv = seq // ATT_TK
    subs = seq // ATT_SUB
    hg = ATT_SHORT_HEADS
    assert n_q % 2 == 0 and MLA_HEADS % hg == 0
    return pl.pallas_call(
        functools.partial(_attn_short_kernel, subs=subs, n_q=n_q),
        grid=(batch, MLA_HEADS // hg),
        in_specs=[
            pl.BlockSpec((hg, n_q, HEAD_PAD, ATT_TQ), lambda b, h: (h, b, 0, 0)),
            pl.BlockSpec((hg, seq, HEAD_PAD), lambda b, h: (h, b, 0)),
            pl.BlockSpec((hg, n_kv, VT_ROWS, ATT_TK), lambda b, h: (h, b, 0, 0)),
        ],
        out_specs=pl.BlockSpec((seq, hg * MLA_V), lambda b, h: (b, h)),
        out_shape=jax.ShapeDtypeStruct((n_tok, MLA_HEADS * MLA_V), BF16),
        scratch_shapes=[pltpu.VMEM((seq, ATT_TQ), F32), pltpu.VMEM((seq, ATT_TQ), F32)],
        compiler_params=_params(("parallel", "parallel")),
        name="mla_attention_short",
    )(q, k, vt)


def attention(q, k, vt, batch, seq):
    if seq <= ATT_SUBS * ATT_SUB:
        return attention_short(q, k, vt, batch, seq)
    n_tok = k.shape[1]
    n_q = min(ATT_NQ, seq // ATT_TQ)
    nq = seq // (n_q * ATT_TQ)
    n_kv = seq // ATT_TK
    subs = ATT_SUBS
    span = subs * ATT_SUB
    n_span = seq // span
    assert n_span % 2 == 0 and span % ATT_TK == 0
    return pl.pallas_call(
        functools.partial(_attn_kernel, n_span=n_span, subs=subs, n_q=n_q),
        grid=(batch, MLA_HEADS, nq),
        in_specs=[
            pl.BlockSpec((None, n_q, HEAD_PAD, ATT_TQ), lambda b, h, i: (h, b * nq + i, 0, 0)),
            pl.BlockSpec((None, seq, HEAD_PAD), lambda b, h, i: (h, b, 0)),
            pl.BlockSpec((None, n_kv, VT_ROWS, ATT_TK), lambda b, h, i: (h, b, 0, 0)),
        ],
        out_specs=pl.BlockSpec((n_q * ATT_TQ, MLA_V), lambda b, h, i: (b * nq + i, h)),
        out_shape=jax.ShapeDtypeStruct((n_tok, MLA_HEADS * MLA_V), BF16),
        scratch_shapes=[pltpu.VMEM((span, ATT_TQ), F32), pltpu.VMEM((span, ATT_TQ), F32)],
        compiler_params=_params(("parallel", "parallel", "arbitrary")),
        name="mla_attention",
    )(q, k, vt)


def _prep_gla(w_in, w_gate_up, b_gate, g_out, w_out):
    w_in_p = jnp.pad(w_in, ((0, 0), (0, GLA_IN_PAD - GLA_IN_W))).astype(BF16)
    wg = jnp.zeros((2, ROPE_LANES, GLA_KEY_W), F32)
    wg = wg.at[0, 0:GLA_GATE_RANK].set(w_gate_up[0])
    wg = wg.at[1, GLA_GATE_RANK:2 * GLA_GATE_RANK].set(w_gate_up[1])
    return dict(w_in=w_in_p, wg=wg.astype(BF16), bg=b_gate.reshape(2, 1, GLA_KEY_W),
                g_out=g_out.reshape(1, GLA_VAL_W), w_out=w_out.astype(BF16))


def _prep_mla(w_in, g_q, w_uq, g_kv, w_ukv, w_out):
    c0 = MLA_Q_RANK + MLA_KV_RANK
    half = MLA_ROPE // 2
    zpad = jnp.zeros((D_MODEL, ROPE_LANES - MLA_ROPE), F32)
    w_in_ext = jnp.concatenate(
        [w_in, zpad, -w_in[:, c0 + half:c0 + MLA_ROPE], w_in[:, c0:c0 + half], zpad], axis=1).astype(BF16)
    uq = w_uq.reshape(MLA_Q_RANK, MLA_HEADS, MLA_QK)
    rope = uq[:, :, MLA_NOPE:]
    wn = uq[:, :, :MLA_NOPE].reshape(MLA_Q_RANK, -1).T.astype(BF16)
    wr = rope.reshape(MLA_Q_RANK, -1).T.astype(BF16)
    wrr = jnp.concatenate([-rope[:, :, half:], rope[:, :, :half]], axis=2).reshape(MLA_Q_RANK, -1).T.astype(BF16)
    ukv = w_ukv.reshape(MLA_KV_RANK, MLA_HEADS, MLA_NOPE + MLA_V)
    wk = ukv[:, :, :MLA_NOPE].reshape(MLA_KV_RANK, -1).astype(BF16)
    wvt = ukv[:, :, MLA_NOPE:].reshape(MLA_KV_RANK, -1).T.astype(BF16)
    freqs = ROPE_THETA ** (-jnp.arange(half, dtype=F32) / half)
    freqs = jnp.tile(freqs, ROPE_LANES // half).reshape(1, ROPE_LANES)
    return dict(w_in=w_in_ext, g_q=g_q.reshape(1, -1), g_kv=g_kv.reshape(1, -1), freqs=freqs,
                wn=wn, wr=wr, wrr=wrr, wk=wk, wvt=wvt, w_out=w_out.astype(BF16))


def _prep_ffn(w_up, conv_w, conv_b, w_down):
    def split(m):
        lead = m.shape[:-1]
        a = m[..., :D_FF].reshape(*lead, N_FF_CHUNKS, FF_CHUNK)
        g = m[..., D_FF:].reshape(*lead, N_FF_CHUNKS, FF_CHUNK)
        return jnp.concatenate([a, g], axis=-1)

    wup = jnp.transpose(split(w_up), (1, 0, 2)).astype(BF16)
    taps = jnp.transpose(split(conv_w.reshape(3, 2 * D_FF)), (1, 0, 2))
    bias = split(conv_b.reshape(1, 2 * D_FF)).transpose(1, 0, 2)
    cpar = jnp.concatenate([taps, bias, jnp.zeros((N_FF_CHUNKS, 4, 2 * FF_CHUNK), F32)], axis=1)
    wdown = w_down.reshape(N_FF_CHUNKS, FF_CHUNK, D_MODEL).astype(BF16)
    return dict(wup=wup, cpar=cpar, wdown=wdown)


def _trunk(x3, norm_mix, norm_ffn, norm_final, gla, mla, ffn):
    batch, seq, d = x3.shape
    x = x3.reshape(batch * seq, d)

    proj = norm_matmul(x, norm_mix[0:1], gla["w_in"], "gla_in_proj")
    o_bwd = gla_pass(proj, gla["wg"][1], gla["bg"][1], batch, seq, bwd=True)
    x = gla_pass(proj, gla["wg"][0], gla["bg"][0], batch, seq, bwd=False,
                 extra=(o_bwd, gla["g_out"], gla["w_out"], x))
    x = conv_ffn(x, norm_ffn[0:1], ffn[0]["wup"], ffn[0]["cpar"], ffn[0]["wdown"], seq)

    proj = norm_matmul(x, norm_mix[1:2], mla["w_in"], "mla_in_proj")
    q, k, vt = mla_qkv(proj, mla["g_q"], mla["g_kv"], mla["freqs"], mla["wn"], mla["wr"], mla["wrr"],
                       mla["wk"], mla["wvt"], seq)
    att = attention(q, k, vt, batch, seq)
    x = matmul_residual(x, att, mla["w_out"], "mla_out_proj")
    x = conv_ffn(x, norm_ffn[1:2], ffn[1]["wup"], ffn[1]["cpar"], ffn[1]["wdown"], seq,
                 g_final=norm_final.reshape(1, d))
    return x.reshape(batch, seq, d)


def kernel(x_prompt, x_sample, norm_mix, norm_ffn, norm_final, gla_w_in, gla_w_gate_up, gla_b_gate, gla_g_out, gla_w_out, mla_w_in, mla_g_q, mla_w_uq, mla_g_kv, mla_w_ukv, mla_w_out, ffn_w_up, ffn_conv_w, ffn_conv_b, ffn_w_down):
    gla = _prep_gla(gla_w_in[0], gla_w_gate_up[0], gla_b_gate[0], gla_g_out[0], gla_w_out[0])
    mla = _prep_mla(mla_w_in[0], mla_g_q[0], mla_w_uq[0], mla_g_kv[0], mla_w_ukv[0], mla_w_out[0])
    ffn = [_prep_ffn(ffn_w_up[i], ffn_conv_w[i], ffn_conv_b[i], ffn_w_down[i]) for i in range(2)]
    y_prompt = _trunk(x_prompt, norm_mix, norm_ffn, norm_final, gla, mla, ffn)
    y_sample = _trunk(x_sample, norm_mix, norm_ffn, norm_final, gla, mla, ffn)
    return (y_prompt, y_sample)
```

```python
import functools
import math

import jax
import jax.numpy as jnp
from jax import lax
from jax.experimental import pallas as pl
from jax.experimental.pallas import tpu as pltpu

F32 = jnp.float32
BF16 = jnp.bfloat16

D_MODEL = 1024
EPS = 1e-6

GLA_HEADS = 4
GLA_DK = 128
GLA_DV = 256
GLA_KEY_W = GLA_HEADS * GLA_DK
GLA_VAL_W = GLA_HEADS * GLA_DV
GLA_GATE_RANK = 16
GLA_GATE_TAU = 16.0
GLA_CHUNK = 64
GLA_IN_W = 2 * GLA_KEY_W + 2 * GLA_VAL_W + 2 * GLA_GATE_RANK
GLA_IN_PAD = 3200
GLA_GATE_COL = 2 * GLA_KEY_W + 2 * GLA_VAL_W

MLA_HEADS = 16
MLA_Q_RANK = 384
MLA_KV_RANK = 256
MLA_NOPE = 128
MLA_ROPE = 64
MLA_V = 128
MLA_QK = MLA_NOPE + MLA_ROPE
MLA_IN_W = MLA_Q_RANK + MLA_KV_RANK + MLA_ROPE
MLA_IN_EXT = 896
ROPE_THETA = 10000.0
ROPE_LANES = 128
HEAD_PAD = 256
BF16_SUBLANES = 16
VT_ROWS = MLA_V + BF16_SUBLANES

D_FF = 2816
FF_CHUNK = 256
N_FF_CHUNKS = D_FF // FF_CHUNK
FF_AHEAD = 3

SUBLANES = 8
VMEM_LIMIT = 56 * 1024 * 1024

TOK_TILE = 512
FFN_TILE = 256
GLA_TILE = 512
GLA_BLOCK = 256
ATT_TQ = 512
ATT_TK = 512
ATT_SUB = 256
ATT_SUBS = 8
ATT_NQ = 8
ATT_SHORT_HEADS = 2


def _rms(x, g):
    return x * lax.rsqrt(jnp.mean(x * x, axis=-1, keepdims=True) + EPS) * g


def _params(sem):
    return pltpu.CompilerParams(dimension_semantics=sem, vmem_limit_bytes=VMEM_LIMIT)


def _norm_matmul_kernel(x_ref, g_ref, w_ref, o_ref):
    hn = _rms(x_ref[...], g_ref[...]).astype(BF16)
    o_ref[...] = jnp.dot(hn, w_ref[...], preferred_element_type=F32).astype(o_ref.dtype)


def norm_matmul(x, g, w, name):
    n_tok, d = x.shape
    n_out = w.shape[1]
    return pl.pallas_call(
        _norm_matmul_kernel,
        grid=(n_tok // TOK_TILE,),
        in_specs=[
            pl.BlockSpec((TOK_TILE, d), lambda i: (i, 0)),
            pl.BlockSpec((1, d), lambda i: (0, 0)),
            pl.BlockSpec((d, n_out), lambda i: (0, 0)),
        ],
        out_specs=pl.BlockSpec((TOK_TILE, n_out), lambda i: (i, 0)),
        out_shape=jax.ShapeDtypeStruct((n_tok, n_out), BF16),
        compiler_params=_params(("parallel",)),
        name=name,
    )(x, g, w)


def _matmul_residual_kernel(x_ref, a_ref, w_ref, o_ref):
    o_ref[...] = x_ref[...] + jnp.dot(a_ref[...], w_ref[...], preferred_element_type=F32)


def matmul_residual(x, a, w, name):
    n_tok, d = x.shape
    k = a.shape[1]
    return pl.pallas_call(
        _matmul_residual_kernel,
        grid=(n_tok // TOK_TILE,),
        in_specs=[
            pl.BlockSpec((TOK_TILE, d), lambda i: (i, 0)),
            pl.BlockSpec((TOK_TILE, k), lambda i: (i, 0)),
            pl.BlockSpec((k, d), lambda i: (0, 0)),
        ],
        out_specs=pl.BlockSpec((TOK_TILE, d), lambda i: (i, 0)),
        out_shape=jax.ShapeDtypeStruct((n_tok, d), F32),
        compiler_params=_params(("parallel",)),
        name=name,
    )(x, a, w)


def _gla_kernel(*refs, bwd, final):
    if final:
        (q_ref, k_ref, v_ref, g_ref, wg_ref, bg_ref, r_ref, ob_ref, gout_ref, wout_ref, x_ref,
         o_ref, state_ref, gated_ref) = refs
    else:
        q_ref, k_ref, v_ref, g_ref, wg_ref, bg_ref, o_ref, state_ref = refs
    t = GLA_BLOCK
    n_chunks = t // GLA_CHUNK
    blocks = range(GLA_TILE // t)

    @pl.when(pl.program_id(1) == 0)
    def _():
        state_ref[...] = jnp.zeros_like(state_ref)

    rows = lax.broadcasted_iota(jnp.int32, (t, t), 0)
    cols = lax.broadcasted_iota(jnp.int32, (t, t), 1)
    shift = GLA_CHUNK.bit_length() - 1
    same = (rows >> shift) == (cols >> shift)
    if bwd:
        cum_m = same & (cols >= rows)
        att_m = same & (cols > rows)
    else:
        cum_m = same & (cols <= rows)
        att_m = same & (cols <= rows)
    cum_mat = jnp.where(cum_m, 1.0, 0.0).astype(BF16)
    tot_mat = jnp.where(same, 1.0, 0.0).astype(BF16)

    blk_rows = [slice(i * t, (i + 1) * t) for i in blocks]
    z = [jnp.dot(g_ref[rs, :], wg_ref[...], preferred_element_type=F32) + bg_ref[...] for rs in blk_rows]
    la = [-(jnp.maximum(-zi, 0.0) + jnp.log(1.0 + jnp.exp(-jnp.abs(zi)))) * (1.0 / GLA_GATE_TAU) for zi in z]
    la_hi = [x.astype(BF16) for x in la]
    la_lo = [(x - h.astype(F32)).astype(BF16) for x, h in zip(la, la_hi)]
    cum = [jnp.dot(cum_mat, h, preferred_element_type=F32) + jnp.dot(cum_mat, l, preferred_element_type=F32)
           for h, l in zip(la_hi, la_lo)]
    tot = [jnp.dot(tot_mat, h, preferred_element_type=F32) + jnp.dot(tot_mat, l, preferred_element_type=F32)
           for h, l in zip(la_hi, la_lo)]

    q = [q_ref[rs, :].astype(F32) * (GLA_DK ** -0.5) for rs in blk_rows]
    k = [k_ref[rs, :].astype(F32) for rs in blk_rows]
    q_dec = [(q[i] * jnp.exp(cum[i])).astype(BF16) for i in blocks]
    k_inv = [(k[i] * jnp.exp(-cum[i])).astype(BF16) for i in blocks]
    k_end = [(k[i] * jnp.exp(tot[i] - cum[i])).astype(BF16) for i in blocks]
    decay = [jnp.exp(tot[i]) for i in blocks]

    heads = range(GLA_HEADS)
    ks = [slice(h * GLA_DK, (h + 1) * GLA_DK) for h in heads]
    vs = [slice(h * GLA_DV, (h + 1) * GLA_DV) for h in heads]
    nt_dims = (((1,), (1,)), ((), ()))
    o_intra = [[None] * GLA_HEADS for _ in blocks]
    for h in heads:
        for i in blocks:
            sc = lax.dot_general(q_dec[i][:, ks[h]], k_inv[i][:, ks[h]], nt_dims, preferred_element_type=F32)
            p = jnp.where(att_m, sc, 0.0).astype(BF16)
            o_intra[i][h] = jnp.dot(p, v_ref[blk_rows[i], vs[h]], preferred_element_type=F32)

    st = [state_ref[h] for h in heads]
    for i in (reversed(blocks) if bwd else blocks):
        for n in (range(n_chunks - 1, -1, -1) if bwd else range(n_chunks)):
            cs = slice(n * GLA_CHUNK, (n + 1) * GLA_CHUNK)
            rs = slice(i * t + n * GLA_CHUNK, i * t + (n + 1) * GLA_CHUNK)
            for h in heads:
                o_inter = lax.dot_general(q_dec[i][cs, ks[h]], st[h].astype(BF16), nt_dims,
                                          preferred_element_type=F32)
                kv_t = lax.dot_general(v_ref[rs, vs[h]], k_end[i][cs, ks[h]], (((0,), (0,)), ((), ())),
                                       preferred_element_type=F32)
                st[h] = decay[i][n * GLA_CHUNK:n * GLA_CHUNK + 1, ks[h]] * st[h] + kv_t
                o_n = o_intra[i][h][cs] + o_inter
                if final:
                    o_n = o_n + ob_ref[rs, vs[h]]
                    o_n = o_n * lax.rsqrt(jnp.mean(o_n * o_n, axis=-1, keepdims=True) + EPS) * gout_ref[:, vs[h]]
                    r = r_ref[rs, vs[h]].astype(F32)
                    o_n = o_n * (r / (1.0 + jnp.exp(-r)))
                    gated_ref[rs, vs[h]] = o_n.astype(BF16)
                else:
                    o_ref[rs, vs[h]] = o_n
    for h in heads:
        state_ref[h] = st[h]

    if final:
        o_ref[...] = x_ref[...] + jnp.dot(gated_ref[...], wout_ref[...], preferred_element_type=F32)


def gla_pass(proj, wg, bg, batch, seq, *, bwd, extra=None):
    n_tok = proj.shape[0]
    t = GLA_TILE
    nb = seq // t
    final = extra is not None

    def row(b, j):
        return b * nb + ((nb - 1 - j) if bwd else j)

    def col(c):
        return lambda b, j: (row(b, j), c)

    in_specs = [
        pl.BlockSpec((t, GLA_KEY_W), col(0)),
        pl.BlockSpec((t, GLA_KEY_W), col(1)),
        pl.BlockSpec((t, GLA_VAL_W), col(1)),
        pl.BlockSpec((t, ROPE_LANES), col(GLA_GATE_COL // ROPE_LANES)),
        pl.BlockSpec((ROPE_LANES, GLA_KEY_W), lambda b, j: (0, 0)),
        pl.BlockSpec((1, GLA_KEY_W), lambda b, j: (0, 0)),
    ]
    args = [proj, proj, proj, proj, wg, bg]
    scratch = [pltpu.VMEM((GLA_HEADS, GLA_DV, GLA_DK), F32)]
    if final:
        o_bwd, g_out, w_out, x = extra
        in_specs += [
            pl.BlockSpec((t, GLA_VAL_W), col(2)),
            pl.BlockSpec((t, GLA_VAL_W), col(0)),
            pl.BlockSpec((1, GLA_VAL_W), lambda b, j: (0, 0)),
            pl.BlockSpec((GLA_VAL_W, D_MODEL), lambda b, j: (0, 0)),
            pl.BlockSpec((t, D_MODEL), col(0)),
        ]
        args += [proj, o_bwd, g_out, w_out, x]
        scratch.append(pltpu.VMEM((t, GLA_VAL_W), BF16))
    return pl.pallas_call(
        functools.partial(_gla_kernel, bwd=bwd, final=final),
        grid=(batch, nb),
        in_specs=in_specs,
        out_specs=pl.BlockSpec((t, GLA_VAL_W), col(0)),
        out_shape=jax.ShapeDtypeStruct((n_tok, GLA_VAL_W), F32),
        scratch_shapes=scratch,
        compiler_params=_params(("parallel", "arbitrary")),
        name="gla_fwd_out" if final else "gla_bwd",
    )(*args)


def _ffn_kernel(*refs, tiles_per_seq, final):
    if final:
        (x_ref, xp_ref, xn_ref, g_ref, wup_ref, cpar_ref, wdown_ref, gfin_ref, o_ref,
         hn_ref, hnb_ref, acc_ref) = refs
    else:
        x_ref, xp_ref, xn_ref, g_ref, wup_ref, cpar_ref, wdown_ref, o_ref, hn_ref, hnb_ref, acc_ref = refs
    t = FFN_TILE
    ext = t + 2 * SUBLANES
    i = pl.program_id(0)
    g = g_ref[...]
    keep_prev = jnp.where(i % tiles_per_seq == 0, 0.0, 1.0)
    keep_next = jnp.where((i + 1) % tiles_per_seq == 0, 0.0, 1.0)
    hn_ref[0:SUBLANES, :] = _rms(xp_ref[...], g) * keep_prev
    hn_ref[SUBLANES:SUBLANES + t, :] = _rms(x_ref[...], g)
    hn_ref[SUBLANES + t:ext, :] = _rms(xn_ref[...], g) * keep_next
    hnb_ref[...] = hn_ref[...].astype(BF16)
    acc_ref[...] = x_ref[...]

    def up(c):
        return jnp.dot(hnb_ref[...], wup_ref[c], preferred_element_type=F32)

    ahead = [up(c) for c in range(FF_AHEAD)]
    for c in range(N_FF_CHUNKS):
        u = ahead.pop(0)
        if c + FF_AHEAD < N_FF_CHUNKS:
            ahead.append(up(c + FF_AHEAD))
        cp = cpar_ref[c]
        conv = (cp[0:1] * pltpu.roll(u, 1, 0) + cp[1:2] * u
                + cp[2:3] * pltpu.roll(u, ext - 1, 0) + cp[3:4])
        conv = conv[SUBLANES:SUBLANES + t]
        a = conv[:, :FF_CHUNK]
        gt = conv[:, FF_CHUNK:]
        act = (a * (gt / (1.0 + jnp.exp(-gt)))).astype(BF16)
        acc_ref[...] += jnp.dot(act, wdown_ref[c], preferred_element_type=F32)
    out = acc_ref[...]
    if final:
        out = _rms(out, gfin_ref[...])
    o_ref[...] = out


def conv_ffn(x, g, wup, cpar, wdown, seq, g_final=None):
    n_tok, d = x.shape
    t = FFN_TILE
    r8 = t // SUBLANES
    last8 = n_tok // SUBLANES - 1
    final = g_final is not None
    const3 = lambda i: (0, 0, 0)
    in_specs = [
        pl.BlockSpec((t, d), lambda i: (i, 0)),
        pl.BlockSpec((SUBLANES, d), lambda i: (jnp.maximum(i * r8 - 1, 0), 0)),
        pl.BlockSpec((SUBLANES, d), lambda i: (jnp.minimum((i + 1) * r8, last8), 0)),
        pl.BlockSpec((1, d), lambda i: (0, 0)),
        pl.BlockSpec(wup.shape, const3, pipeline_mode=pl.Buffered(1)),
        pl.BlockSpec(cpar.shape, const3, pipeline_mode=pl.Buffered(1)),
        pl.BlockSpec(wdown.shape, const3, pipeline_mode=pl.Buffered(1)),
    ]
    args = [x, x, x, g, wup, cpar, wdown]
    if final:
        in_specs.append(pl.BlockSpec((1, d), lambda i: (0, 0)))
        args.append(g_final)
    return pl.pallas_call(
        functools.partial(_ffn_kernel, tiles_per_seq=seq // t, final=final),
        grid=(n_tok // t,),
        in_specs=in_specs,
        out_specs=pl.BlockSpec((t, d), lambda i: (i, 0)),
        out_shape=jax.ShapeDtypeStruct((n_tok, d), F32),
        scratch_shapes=[pltpu.VMEM((t + 2 * SUBLANES, d), F32), pltpu.VMEM((t + 2 * SUBLANES, d), BF16),
                        pltpu.VMEM((t, d), F32)],
        compiler_params=_params(("parallel",)),
        name="conv_ffn_final" if final else "conv_ffn",
    )(*args)


def _mla_qkv_kernel(x_ref, gmix_ref, win_ref, gq_ref, gkv_ref, freq_ref, wn_ref, wr_ref, wrr_ref, wk_ref, wvt_ref,
                    q_ref, k_ref, vt_ref, *, tiles_per_seq):
    t = TOK_TILE
    i = pl.program_id(0)
    pos0 = (i % tiles_per_seq) * t
    pos = (pos0 + lax.broadcasted_iota(jnp.int32, (t, ROPE_LANES), 0)).astype(F32)
    ang = pos * freq_ref[...]
    cos = jnp.cos(ang)
    sin = jnp.sin(ang)

    hn = _rms(x_ref[...], gmix_ref[...]).astype(BF16)
    proj = jnp.dot(hn, win_ref[...], preferred_element_type=F32)
    cq = _rms(proj[:, 0:MLA_Q_RANK], gq_ref[...]).astype(BF16)
    ckv = _rms(proj[:, MLA_Q_RANK:MLA_Q_RANK + MLA_KV_RANK], gkv_ref[...]).astype(BF16)
    c0 = MLA_Q_RANK + MLA_KV_RANK
    kr = (proj[:, c0:c0 + ROPE_LANES] * cos + proj[:, c0 + ROPE_LANES:c0 + 2 * ROPE_LANES] * sin).astype(BF16)

    scale = (MLA_QK ** -0.5) * math.log2(math.e)
    nt_dims = (((1,), (1,)), ((), ()))
    cos_t = cos.T
    sin_t = sin.T
    qn = lax.dot_general(wn_ref[...], cq, nt_dims, preferred_element_type=F32) * scale
    qr = lax.dot_general(wr_ref[...], cq, nt_dims, preferred_element_type=F32)
    qrr = lax.dot_general(wrr_ref[...], cq, nt_dims, preferred_element_type=F32)
    kn = jnp.dot(ckv, wk_ref[...], preferred_element_type=F32)
    vt = lax.dot_general(wvt_ref[...], ckv, nt_dims, preferred_element_type=F32)
    for h in range(MLA_HEADS):
        hs = slice(h * MLA_NOPE, (h + 1) * MLA_NOPE)
        q_ref[h, 0, 0:MLA_NOPE, :] = qn[hs, :].astype(BF16)
        rr = slice(h * MLA_ROPE, (h + 1) * MLA_ROPE)
        q_ref[h, 0, MLA_NOPE:MLA_QK, :] = ((qr[rr, :] * cos_t[0:MLA_ROPE] + qrr[rr, :] * sin_t[0:MLA_ROPE])
                                           * scale).astype(BF16)
        q_ref[h, 0, MLA_QK:HEAD_PAD, :] = jnp.zeros((HEAD_PAD - MLA_QK, t), BF16)
        k_ref[h, :, 0:MLA_NOPE] = kn[:, hs].astype(BF16)
        k_ref[h, :, MLA_NOPE:HEAD_PAD] = kr
        vt_ref[h, 0, 0:MLA_V] = vt[hs, :].astype(BF16)
        vt_ref[h, 0, MLA_V:VT_ROWS] = jnp.ones((VT_ROWS - MLA_V, t), BF16)


def mla_qkv(x, gmix, w_in, gq, gkv, freqs, wn, wr, wrr, wk, wvt, seq):
    n_tok, d = x.shape
    t = TOK_TILE
    nt = n_tok // t
    c2 = lambda i: (0, 0)
    return pl.pallas_call(
        functools.partial(_mla_qkv_kernel, tiles_per_seq=seq // t),
        grid=(nt,),
        in_specs=[
            pl.BlockSpec((t, d), lambda i: (i, 0)),
            pl.BlockSpec(gmix.shape, c2),
            pl.BlockSpec(w_in.shape, c2),
            pl.BlockSpec(gq.shape, c2),
            pl.BlockSpec(gkv.shape, c2),
            pl.BlockSpec(freqs.shape, c2),
            pl.BlockSpec(wn.shape, c2),
            pl.BlockSpec(wr.shape, c2),
            pl.BlockSpec(wrr.shape, c2),
            pl.BlockSpec(wk.shape, c2),
            pl.BlockSpec(wvt.shape, c2),
        ],
        out_specs=[
            pl.BlockSpec((MLA_HEADS, 1, HEAD_PAD, t), lambda i: (0, i, 0, 0)),
            pl.BlockSpec((MLA_HEADS, t, HEAD_PAD), lambda i: (0, i, 0)),
            pl.BlockSpec((MLA_HEADS, 1, VT_ROWS, t), lambda i: (0, i, 0, 0)),
        ],
        out_shape=[
            jax.ShapeDtypeStruct((MLA_HEADS, nt, HEAD_PAD, t), BF16),
            jax.ShapeDtypeStruct((MLA_HEADS, n_tok, HEAD_PAD), BF16),
            jax.ShapeDtypeStruct((MLA_HEADS, nt, VT_ROWS, t), BF16),
        ],
        compiler_params=_params(("parallel",)),
        name="mla_qkv",
    )(x, gmix, w_in, gq, gkv, freqs, wn, wr, wrr, wk, wvt)


def _attn_kernel(q_ref, k_ref, vt_ref, o_ref, sa_ref, sb_ref, *, n_span, subs, n_q):
    span = subs * ATT_SUB

    def qk_block(qi, j, b, dst_ref, cm):
        row0 = pl.multiple_of(j * span + b * ATT_SUB, ATT_SUB)
        s = jnp.dot(k_ref[pl.ds(row0, ATT_SUB), :], q_ref[qi], preferred_element_type=F32)
        dst_ref[b * ATT_SUB:(b + 1) * ATT_SUB, :] = s
        c = jnp.max(s, axis=0, keepdims=True)
        return c if cm is None else jnp.maximum(cm, c)

    def pv_block(j, b, src_ref, m_new, pv):
        p = jnp.exp2(src_ref[b * ATT_SUB:(b + 1) * ATT_SUB, :] - m_new).astype(BF16)
        blk, off = divmod(b * ATT_SUB, ATT_TK)
        vt = vt_ref[j * (span // ATT_TK) + blk, :, off:off + ATT_SUB]
        d = jnp.dot(vt, p, preferred_element_type=F32)
        return d if pv is None else pv + d

    def step(j, cur_ref, nxt_ref, q_next, j_next, cm, m, acc):
        m_new = jnp.maximum(m, cm)
        alpha = jnp.exp2(m - m_new)
        pv = None
        cm_next = None
        for b in range(subs):
            cm_next = qk_block(q_next, j_next, b, nxt_ref, cm_next)
            pv = pv_block(j, b, cur_ref, m_new, pv)
        return cm_next, m_new, alpha * acc + pv

    def finish(qi, acc):
        row0 = pl.multiple_of(qi * ATT_TQ, ATT_TQ)
        o_ref[pl.ds(row0, ATT_TQ), :] = (acc[:MLA_V] / acc[MLA_V:MLA_V + 1]).T.astype(o_ref.dtype)

    m0 = jnp.full((1, ATT_TQ), -jnp.inf, F32)
    acc0 = jnp.zeros((VT_ROWS, ATT_TQ), F32)

    def q_tile(qi, cm):
        def pair(jj, carry):
            cm, m, acc = carry
            cm, m, acc = step(2 * jj, sa_ref, sb_ref, qi, 2 * jj + 1, cm, m, acc)
            return step(2 * jj + 1, sb_ref, sa_ref, qi, 2 * jj + 2, cm, m, acc)

        cm, m, acc = lax.fori_loop(0, n_span // 2 - 1, pair, (cm, m0, acc0))
        cm, m, acc = step(n_span - 2, sa_ref, sb_ref, qi, n_span - 1, cm, m, acc)
        cm, _, acc = step(n_span - 1, sb_ref, sa_ref, jnp.minimum(qi + 1, n_q - 1), 0, cm, m, acc)
        finish(qi, acc)
        return cm

    cm = None
    for b in range(subs):
        cm = qk_block(0, 0, b, sa_ref, cm)
    lax.fori_loop(0, n_q, q_tile, cm)


def _attn_short_kernel(q_ref, k_ref, vt_ref, o_ref, sa_ref, sb_ref, *, subs, n_q):
    def qk_block(hh, qi, b, dst_ref, cm):
        rows = slice(b * ATT_SUB, (b + 1) * ATT_SUB)
        s = jnp.dot(k_ref[hh, rows, :], q_ref[hh, qi], preferred_element_type=F32)
        dst_ref[rows, :] = s
        c = jnp.max(s, axis=0, keepdims=True)
        return c if cm is None else jnp.maximum(cm, c)

    def pv_block(hh, b, src_ref, cm, pv):
        p = jnp.exp2(src_ref[b * ATT_SUB:(b + 1) * ATT_SUB, :] - cm).astype(BF16)
        blk, off = divmod(b * ATT_SUB, ATT_TK)
        d = jnp.dot(vt_ref[hh, blk, :, off:off + ATT_SUB], p, preferred_element_type=F32)
        return d if pv is None else pv + d

    def step(hh, qi, cur_ref, nxt_ref, hh_next, q_next, cm):
        pv = None
        cm_next = None
        for b in range(subs):
            cm_next = qk_block(hh_next, q_next, b, nxt_ref, cm_next)
            pv = pv_block(hh, b, cur_ref, cm, pv)
        row0 = pl.multiple_of(qi * ATT_TQ, ATT_TQ)
        o_ref[pl.ds(row0, ATT_TQ), hh * MLA_V:(hh + 1) * MLA_V] = (
            pv[:MLA_V] / pv[MLA_V:MLA_V + 1]).T.astype(o_ref.dtype)
        return cm_next

    cm = None
    for b in range(subs):
        cm = qk_block(0, 0, b, sa_ref, cm)
    for hh in range(ATT_SHORT_HEADS):
        def pair(i, cm, hh=hh):
            q0 = 2 * i
            cm = step(hh, q0, sa_ref, sb_ref, hh, q0 + 1, cm)
            wrap = q0 + 2 >= n_q
            hh_next = jnp.where(wrap, min(hh + 1, ATT_SHORT_HEADS - 1), hh)
            q_next = jnp.where(wrap, 0, q0 + 2)
            return step(hh, q0 + 1, sb_ref, sa_ref, hh_next, q_next, cm)

        cm = lax.fori_loop(0, n_q // 2, pair, cm)


def attention_short(q, k, vt, batch, seq):
    n_tok = k.shape[1]
    n_q = seq // ATT_TQ
    n_kv = seq // ATT_TK
    subs = seq // ATT_SUB
    hg = ATT_SHORT_HEADS
    assert n_q % 2 == 0 and MLA_HEADS % hg == 0
    return pl.pallas_call(
        functools.partial(_attn_short_kernel, subs=subs, n_q=n_q),
        grid=(batch, MLA_HEADS // hg),
        in_specs=[
            pl.BlockSpec((hg, n_q, HEAD_PAD, ATT_TQ), lambda b, h: (h, b, 0, 0)),
            pl.BlockSpec((hg, seq, HEAD_PAD), lambda b, h: (h, b, 0)),
            pl.BlockSpec((hg, n_kv, VT_ROWS, ATT_TK), lambda b, h: (h, b, 0, 0)),
        ],
        out_specs=pl.BlockSpec((seq, hg * MLA_V), lambda b, h: (b, h)),
        out_shape=jax.ShapeDtypeStruct((n_tok, MLA_HEADS * MLA_V), BF16),
        scratch_shapes=[pltpu.VMEM((seq, ATT_TQ), F32), pltpu.VMEM((seq, ATT_TQ), F32)],
        compiler_params=_params(("parallel", "parallel")),
        name="mla_attention_short",
    )(q, k, vt)


def attention(q, k, vt, batch, seq):
    if seq <= ATT_SUBS * ATT_SUB:
        return attention_short(q, k, vt, batch, seq)
    n_tok = k.shape[1]
    n_q = min(ATT_NQ, seq // ATT_TQ)
    nq = seq // (n_q * ATT_TQ)
    n_kv = seq // ATT_TK
    subs = ATT_SUBS
    span = subs * ATT_SUB
    n_span = seq // span
    assert n_span % 2 == 0 and span % ATT_TK == 0
    return pl.pallas_call(
        functools.partial(_attn_kernel, n_span=n_span, subs=subs, n_q=n_q),
        grid=(batch, MLA_HEADS, nq),
        in_specs=[
            pl.BlockSpec((None, n_q, HEAD_PAD, ATT_TQ), lambda b, h, i: (h, b * nq + i, 0, 0)),
            pl.BlockSpec((None, seq, HEAD_PAD), lambda b, h, i: (h, b, 0)),
            pl.BlockSpec((None, n_kv, VT_ROWS, ATT_TK), lambda b, h, i: (h, b, 0, 0)),
        ],
        out_specs=pl.BlockSpec((n_q * ATT_TQ, MLA_V), lambda b, h, i: (b * nq + i, h)),
        out_shape=jax.ShapeDtypeStruct((n_tok, MLA_HEADS * MLA_V), BF16),
        scratch_shapes=[pltpu.VMEM((span, ATT_TQ), F32), pltpu.VMEM((span, ATT_TQ), F32)],
        compiler_params=_params(("parallel", "parallel", "arbitrary")),
        name="mla_attention",
    )(q, k, vt)


def _prep_gla(w_in, w_gate_up, b_gate, g_out, w_out):
    w_in_p = jnp.pad(w_in, ((0, 0), (0, GLA_IN_PAD - GLA_IN_W))).astype(BF16)
    wg = jnp.zeros((2, ROPE_LANES, GLA_KEY_W), F32)
    wg = wg.at[0, 0:GLA_GATE_RANK].set(w_gate_up[0])
    wg = wg.at[1, GLA_GATE_RANK:2 * GLA_GATE_RANK].set(w_gate_up[1])
    return dict(w_in=w_in_p, wg=wg.astype(BF16), bg=b_gate.reshape(2, 1, GLA_KEY_W),
                g_out=g_out.reshape(1, GLA_VAL_W), w_out=w_out.astype(BF16))


def _prep_mla(w_in, g_q, w_uq, g_kv, w_ukv, w_out):
    c0 = MLA_Q_RANK + MLA_KV_RANK
    half = MLA_ROPE // 2
    zpad = jnp.zeros((D_MODEL, ROPE_LANES - MLA_ROPE), F32)
    w_in_ext = jnp.concatenate(
        [w_in, zpad, -w_in[:, c0 + half:c0 + MLA_ROPE], w_in[:, c0:c0 + half], zpad], axis=1).astype(BF16)
    uq = w_uq.reshape(MLA_Q_RANK, MLA_HEADS, MLA_QK)
    rope = uq[:, :, MLA_NOPE:]
    wn = uq[:, :, :MLA_NOPE].reshape(MLA_Q_RANK, -1).T.astype(BF16)
    wr = rope.reshape(MLA_Q_RANK, -1).T.astype(BF16)
    wrr = jnp.concatenate([-rope[:, :, half:], rope[:, :, :half]], axis=2).reshape(MLA_Q_RANK, -1).T.astype(BF16)
    ukv = w_ukv.reshape(MLA_KV_RANK, MLA_HEADS, MLA_NOPE + MLA_V)
    wk = ukv[:, :, :MLA_NOPE].reshape(MLA_KV_RANK, -1).astype(BF16)
    wvt = ukv[:, :, MLA_NOPE:].reshape(MLA_KV_RANK, -1).T.astype(BF16)
    freqs = ROPE_THETA ** (-jnp.arange(half, dtype=F32) / half)
    freqs = jnp.tile(freqs, ROPE_LANES // half).reshape(1, ROPE_LANES)
    return dict(w_in=w_in_ext, g_q=g_q.reshape(1, -1), g_kv=g_kv.reshape(1, -1), freqs=freqs,
                wn=wn, wr=wr, wrr=wrr, wk=wk, wvt=wvt, w_out=w_out.astype(BF16))


def _prep_ffn(w_up, conv_w, conv_b, w_down):
    def split(m):
        lead = m.shape[:-1]
        a = m[..., :D_FF].reshape(*lead, N_FF_CHUNKS, FF_CHUNK)
        g = m[..., D_FF:].reshape(*lead, N_FF_CHUNKS, FF_CHUNK)
        return jnp.concatenate([a, g], axis=-1)

    wup = jnp.transpose(split(w_up), (1, 0, 2)).astype(BF16)
    taps = jnp.transpose(split(conv_w.reshape(3, 2 * D_FF)), (1, 0, 2))
    bias = split(conv_b.reshape(1, 2 * D_FF)).transpose(1, 0, 2)
    cpar = jnp.concatenate([taps, bias, jnp.zeros((N_FF_CHUNKS, 4, 2 * FF_CHUNK), F32)], axis=1)
    wdown = w_down.reshape(N_FF_CHUNKS, FF_CHUNK, D_MODEL).astype(BF16)
    return dict(wup=wup, cpar=cpar, wdown=wdown)


def _trunk(x3, norm_mix, norm_ffn, norm_final, gla, mla, ffn):
    batch, seq, d = x3.shape
    x = x3.reshape(batch * seq, d)

    proj = norm_matmul(x, norm_mix[0:1], gla["w_in"], "gla_in_proj")
    o_bwd = gla_pass(proj, gla["wg"][1], gla["bg"][1], batch, seq, bwd=True)
    x = gla_pass(proj, gla["wg"][0], gla["bg"][0], batch, seq, bwd=False,
                 extra=(o_bwd, gla["g_out"], gla["w_out"], x))
    x = conv_ffn(x, norm_ffn[0:1], ffn[0]["wup"], ffn[0]["cpar"], ffn[0]["wdown"], seq)

    q, k, vt = mla_qkv(x, norm_mix[1:2], mla["w_in"], mla["g_q"], mla["g_kv"], mla["freqs"], mla["wn"], mla["wr"], mla["wrr"],
                       mla["wk"], mla["wvt"], seq)
    att = attention(q, k, vt, batch, seq)
    x = matmul_residual(x, att, mla["w_out"], "mla_out_proj")
    x = conv_ffn(x, norm_ffn[1:2], ffn[1]["wup"], ffn[1]["cpar"], ffn[1]["wdown"], seq,
                 g_final=norm_final.reshape(1, d))
    return x.reshape(batch, seq, d)


def kernel(x_prompt, x_sample, norm_mix, norm_ffn, norm_final, gla_w_in, gla_w_gate_up, gla_b_gate, gla_g_out, gla_w_out, mla_w_in, mla_g_q, mla_w_uq, mla_g_kv, mla_w_ukv, mla_w_out, ffn_w_up, ffn_conv_w, ffn_conv_b, ffn_w_down):
    gla = _prep_gla(gla_w_in[0], gla_w_gate_up[0], gla_b_gate[0], gla_g_out[0], gla_w_out[0])
    mla = _prep_mla(mla_w_in[0], mla_g_q[0], mla_w_uq[0], mla_g_kv[0], mla_w_ukv[0], mla_w_out[0])
    ffn = [_prep_ffn(ffn_w_up[i], ffn_conv_w[i], ffn_conv_b[i], ffn_w_down[i]) for i in range(2)]
    y_prompt = _trunk(x_prompt, norm_mix, norm_ffn, norm_final, gla, mla, ffn)
    y_sample = _trunk(x_sample, norm_mix, norm_ffn, norm_final, gla, mla, ffn)
    return (y_prompt, y_sample)
```

```python
import functools
import math

import jax
import jax.numpy as jnp
from jax import lax
from jax.experimental import pallas as pl
from jax.experimental.pallas import tpu as pltpu

F32 = jnp.float32
BF16 = jnp.bfloat16

D_MODEL = 1024
EPS = 1e-6

GLA_HEADS = 4
GLA_DK = 128
GLA_DV = 256
GLA_KEY_W = GLA_HEADS * GLA_DK
GLA_VAL_W = GLA_HEADS * GLA_DV
GLA_GATE_RANK = 16
GLA_GATE_TAU = 16.0
GLA_CHUNK = 64
GLA_IN_W = 2 * GLA_KEY_W + 2 * GLA_VAL_W + 2 * GLA_GATE_RANK
GLA_IN_PAD = 3200
GLA_GATE_COL = 2 * GLA_KEY_W + 2 * GLA_VAL_W

MLA_HEADS = 16
MLA_Q_RANK = 384
MLA_KV_RANK = 256
MLA_NOPE = 128
MLA_ROPE = 64
MLA_V = 128
MLA_QK = MLA_NOPE + MLA_ROPE
MLA_IN_W = MLA_Q_RANK + MLA_KV_RANK + MLA_ROPE
MLA_IN_EXT = 896
ROPE_THETA = 10000.0
ROPE_LANES = 128
HEAD_PAD = 256
BF16_SUBLANES = 16
VT_ROWS = MLA_V + BF16_SUBLANES

D_FF = 2816
FF_CHUNK = 256
N_FF_CHUNKS = D_FF // FF_CHUNK
FF_AHEAD = 3

SUBLANES = 8
VMEM_LIMIT = 56 * 1024 * 1024

TOK_TILE = 512
FFN_TILE = 256
GLA_TILE = 512
GLA_BLOCK = 256
ATT_TQ = 512
ATT_TK = 512
ATT_SUB = 256
ATT_SUBS = 8
ATT_NQ = 16
ATT_SHORT_HEADS = 4


def _rms(x, g):
    return x * lax.rsqrt(jnp.mean(x * x, axis=-1, keepdims=True) + EPS) * g


def _params(sem):
    return pltpu.CompilerParams(dimension_semantics=sem, vmem_limit_bytes=VMEM_LIMIT)


def _norm_matmul_kernel(x_ref, g_ref, w_ref, o_ref):
    hn = _rms(x_ref[...], g_ref[...]).astype(BF16)
    o_ref[...] = jnp.dot(hn, w_ref[...], preferred_element_type=F32).astype(o_ref.dtype)


def norm_matmul(x, g, w, name):
    n_tok, d = x.shape
    n_out = w.shape[1]
    return pl.pallas_call(
        _norm_matmul_kernel,
        grid=(n_tok // TOK_TILE,),
        in_specs=[
            pl.BlockSpec((TOK_TILE, d), lambda i: (i, 0)),
            pl.BlockSpec((1, d), lambda i: (0, 0)),
            pl.BlockSpec((d, n_out), lambda i: (0, 0)),
        ],
        out_specs=pl.BlockSpec((TOK_TILE, n_out), lambda i: (i, 0)),
        out_shape=jax.ShapeDtypeStruct((n_tok, n_out), BF16),
        compiler_params=_params(("parallel",)),
        name=name,
    )(x, g, w)


def _matmul_residual_kernel(x_ref, a_ref, w_ref, o_ref):
    o_ref[...] = x_ref[...] + jnp.dot(a_ref[...], w_ref[...], preferred_element_type=F32)


def matmul_residual(x, a, w, name):
    n_tok, d = x.shape
    k = a.shape[1]
    return pl.pallas_call(
        _matmul_residual_kernel,
        grid=(n_tok // TOK_TILE,),
        in_specs=[
            pl.BlockSpec((TOK_TILE, d), lambda i: (i, 0)),
            pl.BlockSpec((TOK_TILE, k), lambda i: (i, 0)),
            pl.BlockSpec((k, d), lambda i: (0, 0)),
        ],
        out_specs=pl.BlockSpec((TOK_TILE, d), lambda i: (i, 0)),
        out_shape=jax.ShapeDtypeStruct((n_tok, d), F32),
        compiler_params=_params(("parallel",)),
        name=name,
    )(x, a, w)


def _gla_kernel(*refs, bwd, final):
    if final:
        (q_ref, k_ref, v_ref, g_ref, wg_ref, bg_ref, r_ref, ob_ref, gout_ref, wout_ref, x_ref,
         o_ref, state_ref, gated_ref) = refs
    else:
        q_ref, k_ref, v_ref, g_ref, wg_ref, bg_ref, o_ref, state_ref = refs
    t = GLA_BLOCK
    n_chunks = t // GLA_CHUNK
    blocks = range(GLA_TILE // t)

    @pl.when(pl.program_id(1) == 0)
    def _():
        state_ref[...] = jnp.zeros_like(state_ref)

    rows = lax.broadcasted_iota(jnp.int32, (t, t), 0)
    cols = lax.broadcasted_iota(jnp.int32, (t, t), 1)
    shift = GLA_CHUNK.bit_length() - 1
    same = (rows >> shift) == (cols >> shift)
    if bwd:
        cum_m = same & (cols >= rows)
        att_m = same & (cols > rows)
    else:
        cum_m = same & (cols <= rows)
        att_m = same & (cols <= rows)
    cum_mat = jnp.where(cum_m, 1.0, 0.0).astype(BF16)
    tot_mat = jnp.where(same, 1.0, 0.0).astype(BF16)

    blk_rows = [slice(i * t, (i + 1) * t) for i in blocks]
    z = [jnp.dot(g_ref[rs, :], wg_ref[...], preferred_element_type=F32) + bg_ref[...] for rs in blk_rows]
    la = [-(jnp.maximum(-zi, 0.0) + jnp.log(1.0 + jnp.exp(-jnp.abs(zi)))) * (1.0 / GLA_GATE_TAU) for zi in z]
    la_hi = [x.astype(BF16) for x in la]
    la_lo = [(x - h.astype(F32)).astype(BF16) for x, h in zip(la, la_hi)]
    cum = [jnp.dot(cum_mat, h, preferred_element_type=F32) + jnp.dot(cum_mat, l, preferred_element_type=F32)
           for h, l in zip(la_hi, la_lo)]
    tot = [jnp.dot(tot_mat, h, preferred_element_type=F32) + jnp.dot(tot_mat, l, preferred_element_type=F32)
           for h, l in zip(la_hi, la_lo)]

    q = [q_ref[rs, :].astype(F32) * (GLA_DK ** -0.5) for rs in blk_rows]
    k = [k_ref[rs, :].astype(F32) for rs in blk_rows]
    q_dec = [(q[i] * jnp.exp(cum[i])).astype(BF16) for i in blocks]
    k_inv = [(k[i] * jnp.exp(-cum[i])).astype(BF16) for i in blocks]
    k_end = [(k[i] * jnp.exp(tot[i] - cum[i])).astype(BF16) for i in blocks]
    decay = [jnp.exp(tot[i]) for i in blocks]

    heads = range(GLA_HEADS)
    ks = [slice(h * GLA_DK, (h + 1) * GLA_DK) for h in heads]
    vs = [slice(h * GLA_DV, (h + 1) * GLA_DV) for h in heads]
    nt_dims = (((1,), (1,)), ((), ()))
    o_intra = [[None] * GLA_HEADS for _ in blocks]
    for h in heads:
        for i in blocks:
            sc = lax.dot_general(q_dec[i][:, ks[h]], k_inv[i][:, ks[h]], nt_dims, preferred_element_type=F32)
            p = jnp.where(att_m, sc, 0.0).astype(BF16)
            o_intra[i][h] = jnp.dot(p, v_ref[blk_rows[i], vs[h]], preferred_element_type=F32)

    st = [state_ref[h] for h in heads]
    for i in (reversed(blocks) if bwd else blocks):
        for n in (range(n_chunks - 1, -1, -1) if bwd else range(n_chunks)):
            cs = slice(n * GLA_CHUNK, (n + 1) * GLA_CHUNK)
            rs = slice(i * t + n * GLA_CHUNK, i * t + (n + 1) * GLA_CHUNK)
            for h in heads:
                o_inter = lax.dot_general(q_dec[i][cs, ks[h]], st[h].astype(BF16), nt_dims,
                                          preferred_element_type=F32)
                kv_t = lax.dot_general(v_ref[rs, vs[h]], k_end[i][cs, ks[h]], (((0,), (0,)), ((), ())),
                                       preferred_element_type=F32)
                st[h] = decay[i][n * GLA_CHUNK:n * GLA_CHUNK + 1, ks[h]] * st[h] + kv_t
                o_n = o_intra[i][h][cs] + o_inter
                if final:
                    o_n = o_n + ob_ref[rs, vs[h]]
                    o_n = o_n * lax.rsqrt(jnp.mean(o_n * o_n, axis=-1, keepdims=True) + EPS) * gout_ref[:, vs[h]]
                    r = r_ref[rs, vs[h]].astype(F32)
                    o_n = o_n * (r / (1.0 + jnp.exp(-r)))
                    gated_ref[rs, vs[h]] = o_n.astype(BF16)
                else:
                    o_ref[rs, vs[h]] = o_n
    for h in heads:
        state_ref[h] = st[h]

    if final:
        o_ref[...] = x_ref[...] + jnp.dot(gated_ref[...], wout_ref[...], preferred_element_type=F32)


def gla_pass(proj, wg, bg, batch, seq, *, bwd, extra=None):
    n_tok = proj.shape[0]
    t = GLA_TILE
    nb = seq // t
    final = extra is not None

    def row(b, j):
        return b * nb + ((nb - 1 - j) if bwd else j)

    def col(c):
        return lambda b, j: (row(b, j), c)

    in_specs = [
        pl.BlockSpec((t, GLA_KEY_W), col(0)),
        pl.BlockSpec((t, GLA_KEY_W), col(1)),
        pl.BlockSpec((t, GLA_VAL_W), col(1)),
        pl.BlockSpec((t, ROPE_LANES), col(GLA_GATE_COL // ROPE_LANES)),
        pl.BlockSpec((ROPE_LANES, GLA_KEY_W), lambda b, j: (0, 0)),
        pl.BlockSpec((1, GLA_KEY_W), lambda b, j: (0, 0)),
    ]
    args = [proj, proj, proj, proj, wg, bg]
    scratch = [pltpu.VMEM((GLA_HEADS, GLA_DV, GLA_DK), F32)]
    if final:
        o_bwd, g_out, w_out, x = extra
        in_specs += [
            pl.BlockSpec((t, GLA_VAL_W), col(2)),
            pl.BlockSpec((t, GLA_VAL_W), col(0)),
            pl.BlockSpec((1, GLA_VAL_W), lambda b, j: (0, 0)),
            pl.BlockSpec((GLA_VAL_W, D_MODEL), lambda b, j: (0, 0)),
            pl.BlockSpec((t, D_MODEL), col(0)),
        ]
        args += [proj, o_bwd, g_out, w_out, x]
        scratch.append(pltpu.VMEM((t, GLA_VAL_W), BF16))
    return pl.pallas_call(
        functools.partial(_gla_kernel, bwd=bwd, final=final),
        grid=(batch, nb),
        in_specs=in_specs,
        out_specs=pl.BlockSpec((t, GLA_VAL_W), col(0)),
        out_shape=jax.ShapeDtypeStruct((n_tok, GLA_VAL_W), F32),
        scratch_shapes=scratch,
        compiler_params=_params(("parallel", "arbitrary")),
        name="gla_fwd_out" if final else "gla_bwd",
    )(*args)


def _ffn_kernel(*refs, tiles_per_seq, final):
    if final:
        (x_ref, xp_ref, xn_ref, g_ref, wup_ref, cpar_ref, wdown_ref, gfin_ref, o_ref,
         hn_ref, hnb_ref, acc_ref) = refs
    else:
        x_ref, xp_ref, xn_ref, g_ref, wup_ref, cpar_ref, wdown_ref, o_ref, hn_ref, hnb_ref, acc_ref = refs
    t = FFN_TILE
    ext = t + 2 * SUBLANES
    i = pl.program_id(0)
    g = g_ref[...]
    keep_prev = jnp.where(i % tiles_per_seq == 0, 0.0, 1.0)
    keep_next = jnp.where((i + 1) % tiles_per_seq == 0, 0.0, 1.0)
    hn_ref[0:SUBLANES, :] = _rms(xp_ref[...], g) * keep_prev
    hn_ref[SUBLANES:SUBLANES + t, :] = _rms(x_ref[...], g)
    hn_ref[SUBLANES + t:ext, :] = _rms(xn_ref[...], g) * keep_next
    hnb_ref[...] = hn_ref[...].astype(BF16)
    acc_ref[...] = x_ref[...]

    def up(c):
        return jnp.dot(hnb_ref[...], wup_ref[c], preferred_element_type=F32)

    ahead = [up(c) for c in range(FF_AHEAD)]
    for c in range(N_FF_CHUNKS):
        u = ahead.pop(0)
        if c + FF_AHEAD < N_FF_CHUNKS:
            ahead.append(up(c + FF_AHEAD))
        cp = cpar_ref[c]
        conv = (cp[0:1] * pltpu.roll(u, 1, 0) + cp[1:2] * u
                + cp[2:3] * pltpu.roll(u, ext - 1, 0) + cp[3:4])
        conv = conv[SUBLANES:SUBLANES + t]
        a = conv[:, :FF_CHUNK]
        gt = conv[:, FF_CHUNK:]
        act = (a * (gt / (1.0 + jnp.exp(-gt)))).astype(BF16)
        acc_ref[...] += jnp.dot(act, wdown_ref[c], preferred_element_type=F32)
    out = acc_ref[...]
    if final:
        out = _rms(out, gfin_ref[...])
    o_ref[...] = out


def conv_ffn(x, g, wup, cpar, wdown, seq, g_final=None):
    n_tok, d = x.shape
    t = FFN_TILE
    r8 = t // SUBLANES
    last8 = n_tok // SUBLANES - 1
    final = g_final is not None
    const3 = lambda i: (0, 0, 0)
    in_specs = [
        pl.BlockSpec((t, d), lambda i: (i, 0)),
        pl.BlockSpec((SUBLANES, d), lambda i: (jnp.maximum(i * r8 - 1, 0), 0)),
        pl.BlockSpec((SUBLANES, d), lambda i: (jnp.minimum((i + 1) * r8, last8), 0)),
        pl.BlockSpec((1, d), lambda i: (0, 0)),
        pl.BlockSpec(wup.shape, const3, pipeline_mode=pl.Buffered(1)),
        pl.BlockSpec(cpar.shape, const3, pipeline_mode=pl.Buffered(1)),
        pl.BlockSpec(wdown.shape, const3, pipeline_mode=pl.Buffered(1)),
    ]
    args = [x, x, x, g, wup, cpar, wdown]
    if final:
        in_specs.append(pl.BlockSpec((1, d), lambda i: (0, 0)))
        args.append(g_final)
    return pl.pallas_call(
        functools.partial(_ffn_kernel, tiles_per_seq=seq // t, final=final),
        grid=(n_tok // t,),
        in_specs=in_specs,
        out_specs=pl.BlockSpec((t, d), lambda i: (i, 0)),
        out_shape=jax.ShapeDtypeStruct((n_tok, d), F32),
        scratch_shapes=[pltpu.VMEM((t + 2 * SUBLANES, d), F32), pltpu.VMEM((t + 2 * SUBLANES, d), BF16),
                        pltpu.VMEM((t, d), F32)],
        compiler_params=_params(("parallel",)),
        name="conv_ffn_final" if final else "conv_ffn",
    )(*args)


def _mla_qkv_kernel(x_ref, gmix_ref, win_ref, gq_ref, gkv_ref, freq_ref, wn_ref, wr_ref, wrr_ref, wk_ref, wvt_ref,
                    q_ref, k_ref, vt_ref, *, tiles_per_seq):
    t = TOK_TILE
    i = pl.program_id(0)
    pos0 = (i % tiles_per_seq) * t
    pos = (pos0 + lax.broadcasted_iota(jnp.int32, (t, ROPE_LANES), 0)).astype(F32)
    ang = pos * freq_ref[...]
    cos = jnp.cos(ang)
    sin = jnp.sin(ang)

    hn = _rms(x_ref[...], gmix_ref[...]).astype(BF16)
    proj = jnp.dot(hn, win_ref[...], preferred_element_type=F32)
    cq = _rms(proj[:, 0:MLA_Q_RANK], gq_ref[...]).astype(BF16)
    ckv = _rms(proj[:, MLA_Q_RANK:MLA_Q_RANK + MLA_KV_RANK], gkv_ref[...]).astype(BF16)
    c0 = MLA_Q_RANK + MLA_KV_RANK
    kr = (proj[:, c0:c0 + ROPE_LANES] * cos + proj[:, c0 + ROPE_LANES:c0 + 2 * ROPE_LANES] * sin).astype(BF16)

    scale = (MLA_QK ** -0.5) * math.log2(math.e)
    nt_dims = (((1,), (1,)), ((), ()))
    cos_t = cos.T
    sin_t = sin.T
    qn = lax.dot_general(wn_ref[...], cq, nt_dims, preferred_element_type=F32) * scale
    qr = lax.dot_general(wr_ref[...], cq, nt_dims, preferred_element_type=F32)
    qrr = lax.dot_general(wrr_ref[...], cq, nt_dims, preferred_element_type=F32)
    kn = jnp.dot(ckv, wk_ref[...], preferred_element_type=F32)
    vt = lax.dot_general(wvt_ref[...], ckv, nt_dims, preferred_element_type=F32)
    for h in range(MLA_HEADS):
        hs = slice(h * MLA_NOPE, (h + 1) * MLA_NOPE)
        q_ref[h, 0, 0:MLA_NOPE, :] = qn[hs, :].astype(BF16)
        rr = slice(h * MLA_ROPE, (h + 1) * MLA_ROPE)
        q_ref[h, 0, MLA_NOPE:MLA_QK, :] = ((qr[rr, :] * cos_t[0:MLA_ROPE] + qrr[rr, :] * sin_t[0:MLA_ROPE])
                                           * scale).astype(BF16)
        q_ref[h, 0, MLA_QK:HEAD_PAD, :] = jnp.zeros((HEAD_PAD - MLA_QK, t), BF16)
        k_ref[h, :, 0:MLA_NOPE] = kn[:, hs].astype(BF16)
        k_ref[h, :, MLA_NOPE:HEAD_PAD] = kr
        vt_ref[h, 0, 0:MLA_V] = vt[hs, :].astype(BF16)
        vt_ref[h, 0, MLA_V:VT_ROWS] = jnp.ones((VT_ROWS - MLA_V, t), BF16)


def mla_qkv(x, gmix, w_in, gq, gkv, freqs, wn, wr, wrr, wk, wvt, seq):
    n_tok, d = x.shape
    t = TOK_TILE
    nt = n_tok // t
    c2 = lambda i: (0, 0)
    return pl.pallas_call(
        functools.partial(_mla_qkv_kernel, tiles_per_seq=seq // t),
        grid=(nt,),
        in_specs=[
            pl.BlockSpec((t, d), lambda i: (i, 0)),
            pl.BlockSpec(gmix.shape, c2),
            pl.BlockSpec(w_in.shape, c2),
            pl.BlockSpec(gq.shape, c2),
            pl.BlockSpec(gkv.shape, c2),
            pl.BlockSpec(freqs.shape, c2),
            pl.BlockSpec(wn.shape, c2),
            pl.BlockSpec(wr.shape, c2),
            pl.BlockSpec(wrr.shape, c2),
            pl.BlockSpec(wk.shape, c2),
            pl.BlockSpec(wvt.shape, c2),
        ],
        out_specs=[
            pl.BlockSpec((MLA_HEADS, 1, HEAD_PAD, t), lambda i: (0, i, 0, 0)),
            pl.BlockSpec((MLA_HEADS, t, HEAD_PAD), lambda i: (0, i, 0)),
            pl.BlockSpec((MLA_HEADS, 1, VT_ROWS, t), lambda i: (0, i, 0, 0)),
        ],
        out_shape=[
            jax.ShapeDtypeStruct((MLA_HEADS, nt, HEAD_PAD, t), BF16),
            jax.ShapeDtypeStruct((MLA_HEADS, n_tok, HEAD_PAD), BF16),
            jax.ShapeDtypeStruct((MLA_HEADS, nt, VT_ROWS, t), BF16),
        ],
        compiler_params=_params(("parallel",)),
        name="mla_qkv",
    )(x, gmix, w_in, gq, gkv, freqs, wn, wr, wrr, wk, wvt)


def _attn_kernel(q_ref, k_ref, vt_ref, o_ref, sa_ref, sb_ref, *, n_span, subs, n_q):
    span = subs * ATT_SUB

    def qk_block(qi, j, b, dst_ref, cm):
        row0 = pl.multiple_of(j * span + b * ATT_SUB, ATT_SUB)
        s = jnp.dot(k_ref[pl.ds(row0, ATT_SUB), :], q_ref[qi], preferred_element_type=F32)
        dst_ref[b * ATT_SUB:(b + 1) * ATT_SUB, :] = s
        c = jnp.max(s, axis=0, keepdims=True)
        return c if cm is None else jnp.maximum(cm, c)

    def pv_block(j, b, src_ref, m_new, pv):
        p = jnp.exp2(src_ref[b * ATT_SUB:(b + 1) * ATT_SUB, :] - m_new).astype(BF16)
        blk, off = divmod(b * ATT_SUB, ATT_TK)
        vt = vt_ref[j * (span // ATT_TK) + blk, :, off:off + ATT_SUB]
        d = jnp.dot(vt, p, preferred_element_type=F32)
        return d if pv is None else pv + d

    def step(j, cur_ref, nxt_ref, q_next, j_next, cm, m, acc):
        m_new = jnp.maximum(m, cm)
        alpha = jnp.exp2(m - m_new)
        pv = None
        cm_next = None
        for b in range(subs):
            cm_next = qk_block(q_next, j_next, b, nxt_ref, cm_next)
            pv = pv_block(j, b, cur_ref, m_new, pv)
        return cm_next, m_new, alpha * acc + pv

    def finish(qi, acc):
        row0 = pl.multiple_of(qi * ATT_TQ, ATT_TQ)
        o_ref[pl.ds(row0, ATT_TQ), :] = (acc[:MLA_V] / acc[MLA_V:MLA_V + 1]).T.astype(o_ref.dtype)

    m0 = jnp.full((1, ATT_TQ), -jnp.inf, F32)
    acc0 = jnp.zeros((VT_ROWS, ATT_TQ), F32)

    def q_tile(qi, cm):
        def pair(jj, carry):
            cm, m, acc = carry
            cm, m, acc = step(2 * jj, sa_ref, sb_ref, qi, 2 * jj + 1, cm, m, acc)
            return step(2 * jj + 1, sb_ref, sa_ref, qi, 2 * jj + 2, cm, m, acc)

        cm, m, acc = lax.fori_loop(0, n_span // 2 - 1, pair, (cm, m0, acc0))
        cm, m, acc = step(n_span - 2, sa_ref, sb_ref, qi, n_span - 1, cm, m, acc)
        cm, _, acc = step(n_span - 1, sb_ref, sa_ref, jnp.minimum(qi + 1, n_q - 1), 0, cm, m, acc)
        finish(qi, acc)
        return cm

    cm = None
    for b in range(subs):
        cm = qk_block(0, 0, b, sa_ref, cm)
    lax.fori_loop(0, n_q, q_tile, cm)


def _attn_short_kernel(q_ref, k_ref, vt_ref, o_ref, sa_ref, sb_ref, *, subs, n_q):
    def qk_block(hh, qi, b, dst_ref, cm):
        rows = slice(b * ATT_SUB, (b + 1) * ATT_SUB)
        s = jnp.dot(k_ref[hh, rows, :], q_ref[hh, qi], preferred_element_type=F32)
        dst_ref[rows, :] = s
        c = jnp.max(s, axis=0, keepdims=True)
        return c if cm is None else jnp.maximum(cm, c)

    def pv_block(hh, b, src_ref, cm, pv):
        p = jnp.exp2(src_ref[b * ATT_SUB:(b + 1) * ATT_SUB, :] - cm).astype(BF16)
        blk, off = divmod(b * ATT_SUB, ATT_TK)
        d = jnp.dot(vt_ref[hh, blk, :, off:off + ATT_SUB], p, preferred_element_type=F32)
        return d if pv is None else pv + d

    def step(hh, qi, cur_ref, nxt_ref, hh_next, q_next, cm):
        pv = None
        cm_next = None
        for b in range(subs):
            cm_next = qk_block(hh_next, q_next, b, nxt_ref, cm_next)
            pv = pv_block(hh, b, cur_ref, cm, pv)
        row0 = pl.multiple_of(qi * ATT_TQ, ATT_TQ)
        o_ref[pl.ds(row0, ATT_TQ), hh * MLA_V:(hh + 1) * MLA_V] = (
            pv[:MLA_V] / pv[MLA_V:MLA_V + 1]).T.astype(o_ref.dtype)
        return cm_next

    cm = None
    for b in range(subs):
        cm = qk_block(0, 0, b, sa_ref, cm)
    for hh in range(ATT_SHORT_HEADS):
        def pair(i, cm, hh=hh):
            q0 = 2 * i
            cm = step(hh, q0, sa_ref, sb_ref, hh, q0 + 1, cm)
            wrap = q0 + 2 >= n_q
            hh_next = jnp.where(wrap, min(hh + 1, ATT_SHORT_HEADS - 1), hh)
            q_next = jnp.where(wrap, 0, q0 + 2)
            return step(hh, q0 + 1, sb_ref, sa_ref, hh_next, q_next, cm)

        cm = lax.fori_loop(0, n_q // 2, pair, cm)


def attention_short(q, k, vt, batch, seq):
    n_tok = k.shape[1]
    n_q = seq // ATT_TQ
    n_kv = seq // ATT_TK
    subs = seq // ATT_SUB
    hg = ATT_SHORT_HEADS
    assert n_q % 2 == 0 and MLA_HEADS % hg == 0
    return pl.pallas_call(
        functools.partial(_attn_short_kernel, subs=subs, n_q=n_q),
        grid=(batch, MLA_HEADS // hg),
        in_specs=[
            pl.BlockSpec((hg, n_q, HEAD_PAD, ATT_TQ), lambda b, h: (h, b, 0, 0)),
            pl.BlockSpec((hg, seq, HEAD_PAD), lambda b, h: (h, b, 0)),
            pl.BlockSpec((hg, n_kv, VT_ROWS, ATT_TK), lambda b, h: (h, b, 0, 0)),
        ],
        out_specs=pl.BlockSpec((seq, hg * MLA_V), lambda b, h: (b, h)),
        out_shape=jax.ShapeDtypeStruct((n_tok, MLA_HEADS * MLA_V), BF16),
        scratch_shapes=[pltpu.VMEM((seq, ATT_TQ), F32), pltpu.VMEM((seq, ATT_TQ), F32)],
        compiler_params=_params(("parallel", "parallel")),
        name="mla_attention_short",
    )(q, k, vt)


def attention(q, k, vt, batch, seq):
    if seq <= ATT_SUBS * ATT_SUB:
        return attention_short(q, k, vt, batch, seq)
    n_tok = k.shape[1]
    n_q = min(ATT_NQ, seq // ATT_TQ)
    nq = seq // (n_q * ATT_TQ)
    n_kv = seq // ATT_TK
    subs = ATT_SUBS
    span = subs * ATT_SUB
    n_span = seq // span
    assert n_span % 2 == 0 and span % ATT_TK == 0
    return pl.pallas_call(
        functools.partial(_attn_kernel, n_span=n_span, subs=subs, n_q=n_q),
        grid=(batch, MLA_HEADS, nq),
        in_specs=[
            pl.BlockSpec((None, n_q, HEAD_PAD, ATT_TQ), lambda b, h, i: (h, b * nq + i, 0, 0)),
            pl.BlockSpec((None, seq, HEAD_PAD), lambda b, h, i: (h, b, 0)),
            pl.BlockSpec((None, n_kv, VT_ROWS, ATT_TK), lambda b, h, i: (h, b, 0, 0)),
        ],
        out_specs=pl.BlockSpec((n_q * ATT_TQ, MLA_V), lambda b, h, i: (b * nq + i, h)),
        out_shape=jax.ShapeDtypeStruct((n_tok, MLA_HEADS * MLA_V), BF16),
        scratch_shapes=[pltpu.VMEM((span, ATT_TQ), F32), pltpu.VMEM((span, ATT_TQ), F32)],
        compiler_params=_params(("parallel", "parallel", "arbitrary")),
        name="mla_attention",
    )(q, k, vt)


def _prep_gla(w_in, w_gate_up, b_gate, g_out, w_out):
    w_in_p = jnp.pad(w_in, ((0, 0), (0, GLA_IN_PAD - GLA_IN_W))).astype(BF16)
    wg = jnp.zeros((2, ROPE_LANES, GLA_KEY_W), F32)
    wg = wg.at[0, 0:GLA_GATE_RANK].set(w_gate_up[0])
    wg = wg.at[1, GLA_GATE_RANK:2 * GLA_GATE_RANK].set(w_gate_up[1])
    return dict(w_in=w_in_p, wg=wg.astype(BF16), bg=b_gate.reshape(2, 1, GLA_KEY_W),
                g_out=g_out.reshape(1, GLA_VAL_W), w_out=w_out.astype(BF16))


def _prep_mla(w_in, g_q, w_uq, g_kv, w_ukv, w_out):
    c0 = MLA_Q_RANK + MLA_KV_RANK
    half = MLA_ROPE // 2
    zpad = jnp.zeros((D_MODEL, ROPE_LANES - MLA_ROPE), F32)
    w_in_ext = jnp.concatenate(
        [w_in, zpad, -w_in[:, c0 + half:c0 + MLA_ROPE], w_in[:, c0:c0 + half], zpad], axis=1).astype(BF16)
    uq = w_uq.reshape(MLA_Q_RANK, MLA_HEADS, MLA_QK)
    rope = uq[:, :, MLA_NOPE:]
    wn = uq[:, :, :MLA_NOPE].reshape(MLA_Q_RANK, -1).T.astype(BF16)
    wr = rope.reshape(MLA_Q_RANK, -1).T.astype(BF16)
    wrr = jnp.concatenate([-rope[:, :, half:], rope[:, :, :half]], axis=2).reshape(MLA_Q_RANK, -1).T.astype(BF16)
    ukv = w_ukv.reshape(MLA_KV_RANK, MLA_HEADS, MLA_NOPE + MLA_V)
    wk = ukv[:, :, :MLA_NOPE].reshape(MLA_KV_RANK, -1).astype(BF16)
    wvt = ukv[:, :, MLA_NOPE:].reshape(MLA_KV_RANK, -1).T.astype(BF16)
    freqs = ROPE_THETA ** (-jnp.arange(half, dtype=F32) / half)
    freqs = jnp.tile(freqs, ROPE_LANES // half).reshape(1, ROPE_LANES)
    return dict(w_in=w_in_ext, g_q=g_q.reshape(1, -1), g_kv=g_kv.reshape(1, -1), freqs=freqs,
                wn=wn, wr=wr, wrr=wrr, wk=wk, wvt=wvt, w_out=w_out.astype(BF16))


def _prep_ffn(w_up, conv_w, conv_b, w_down):
    def split(m):
        lead = m.shape[:-1]
        a = m[..., :D_FF].reshape(*lead, N_FF_CHUNKS, FF_CHUNK)
        g = m[..., D_FF:].reshape(*lead, N_FF_CHUNKS, FF_CHUNK)
        return jnp.concatenate([a, g], axis=-1)

    wup = jnp.transpose(split(w_up), (1, 0, 2)).astype(BF16)
    taps = jnp.transpose(split(conv_w.reshape(3, 2 * D_FF)), (1, 0, 2))
    bias = split(conv_b.reshape(1, 2 * D_FF)).transpose(1, 0, 2)
    cpar = jnp.concatenate([taps, bias, jnp.zeros((N_FF_CHUNKS, 4, 2 * FF_CHUNK), F32)], axis=1)
    wdown = w_down.reshape(N_FF_CHUNKS, FF_CHUNK, D_MODEL).astype(BF16)
    return dict(wup=wup, cpar=cpar, wdown=wdown)


def _trunk(x3, norm_mix, norm_ffn, norm_final, gla, mla, ffn):
    batch, seq, d = x3.shape
    x = x3.reshape(batch * seq, d)

    proj = norm_matmul(x, norm_mix[0:1], gla["w_in"], "gla_in_proj")
    o_bwd = gla_pass(proj, gla["wg"][1], gla["bg"][1], batch, seq, bwd=True)
    x = gla_pass(proj, gla["wg"][0], gla["bg"][0], batch, seq, bwd=False,
                 extra=(o_bwd, gla["g_out"], gla["w_out"], x))
    x = conv_ffn(x, norm_ffn[0:1], ffn[0]["wup"], ffn[0]["cpar"], ffn[0]["wdown"], seq)

    q, k, vt = mla_qkv(x, norm_mix[1:2], mla["w_in"], mla["g_q"], mla["g_kv"], mla["freqs"], mla["wn"], mla["wr"], mla["wrr"],
                       mla["wk"], mla["wvt"], seq)
    att = attention(q, k, vt, batch, seq)
    x = matmul_residual(x, att, mla["w_out"], "mla_out_proj")
    x = conv_ffn(x, norm_ffn[1:2], ffn[1]["wup"], ffn[1]["cpar"], ffn[1]["wdown"], seq,
                 g_final=norm_final.reshape(1, d))
    return x.reshape(batch, seq, d)


def kernel(x_prompt, x_sample, norm_mix, norm_ffn, norm_final, gla_w_in, gla_w_gate_up, gla_b_gate, gla_g_out, gla_w_out, mla_w_in, mla_g_q, mla_w_uq, mla_g_kv, mla_w_ukv, mla_w_out, ffn_w_up, ffn_conv_w, ffn_conv_b, ffn_w_down):
    gla = _prep_gla(gla_w_in[0], gla_w_gate_up[0], gla_b_gate[0], gla_g_out[0], gla_w_out[0])
    mla = _prep_mla(mla_w_in[0], mla_g_q[0], mla_w_uq[0], mla_g_kv[0], mla_w_ukv[0], mla_w_out[0])
    ffn = [_prep_ffn(ffn_w_up[i], ffn_conv_w[i], ffn_conv_b[i], ffn_w_down[i]) for i in range(2)]
    y_prompt = _trunk(x_prompt, norm_mix, norm_ffn, norm_final, gla, mla, ffn)
    y_sample = _trunk(x_sample, norm_mix, norm_ffn, norm_final, gla, mla, ffn)
    return (y_prompt, y_sample)
```
